```python
import jax, jax.numpy as jnp
from jax import lax
import numpy as np

D_MODEL = 1024
BATCH = 4
SEQ = 8192
DEPTH = 2

CHUNK = 64
Q_BLOCK = 128
EPS = 1e-6

POOL_WIDTH = D_MODEL // 4
POOL_WINDOWS = (2, 4, 8, 16)
POOL_GROUPS = len(POOL_WINDOWS)
POOL_GROUP = POOL_WIDTH // POOL_GROUPS
GLA_WIDTH = D_MODEL // 4
GLA_HEADS = 4
GLA_HEAD_DIM = GLA_WIDTH // GLA_HEADS
GLA_GATE_RANK = 16
GLA_TAU = 16.0
FOX_WIDTH = D_MODEL - POOL_WIDTH - GLA_WIDTH
FOX_HEAD_DIM = 64
FOX_HEADS = FOX_WIDTH // FOX_HEAD_DIM
D_FF = ((-(-8 * D_MODEL // 3)) + 255) // 256 * 256

IN_SIZES = (POOL_WIDTH,
            GLA_WIDTH, GLA_WIDTH, GLA_WIDTH, GLA_WIDTH,
            GLA_GATE_RANK,
            FOX_WIDTH, FOX_WIDTH, FOX_WIDTH,
            FOX_HEADS)
IN_WIDTH = sum(IN_SIZES)

kernel_name = "hybrid_pool_gla_fox_block"


def _split_points():
    pts, acc = [], 0
    for s in IN_SIZES[:-1]:
        acc += s
        pts.append(acc)
    return pts


def rmsnorm(x, g):
    x32 = x.astype(jnp.float32)
    y = x32 * lax.rsqrt(jnp.mean(x32 * x32, axis=-1, keepdims=True) + EPS) * g.astype(jnp.float32)
    return y.astype(x.dtype)


def pool_mixer(u, w_pool, pool_scale):
    B, T, _ = u.shape
    u32 = u.astype(jnp.float32)
    cs = jnp.cumsum(u32, axis=1)
    cs_pad = jnp.concatenate([jnp.zeros((B, 1, POOL_WIDTH), jnp.float32), cs], axis=1)
    t1 = jnp.arange(1, T + 1)
    means = []
    for gi, w in enumerate(POOL_WINDOWS):
        lo, hi = gi * POOL_GROUP, (gi + 1) * POOL_GROUP
        lagged = jnp.concatenate([jnp.zeros((B, w - 1, POOL_GROUP), jnp.float32),
                                  cs_pad[:, :T - w + 1, lo:hi]], axis=1)
        cnt = jnp.minimum(t1, w).astype(jnp.float32)
        means.append((cs[:, :, lo:hi] - lagged) / cnt[None, :, None])
    d = (jnp.concatenate(means, axis=-1) - u32).astype(u.dtype)
    d = d.reshape(B, T, POOL_GROUPS, POOL_GROUP)
    y = jnp.einsum('btgc,gcd->btgd', d, w_pool).reshape(B, T, POOL_WIDTH)
    return y * pool_scale


def gla_mixer(q, k, v, g, a_low, w_a_up, b_a, gla_gn):
    B, T, _ = q.shape
    H, Dh, C = GLA_HEADS, GLA_HEAD_DIM, CHUNK
    NC = T // C

    def heads(z):
        return z.astype(jnp.float32).reshape(B, NC, C, H, Dh).transpose(0, 3, 1, 2, 4)

    log_a = jax.nn.log_sigmoid(jnp.einsum('btr,rd->btd', a_low.astype(jnp.float32),
                                          w_a_up.astype(jnp.float32)) + b_a.astype(jnp.float32)) / GLA_TAU
    qh = heads(q) * (Dh ** -0.5)
    kh, vh, la = heads(k), heads(v), heads(log_a)
    bcum = jnp.cumsum(la, axis=3)
    b_last = bcum[:, :, :, -1:, :]
    ref = bcum[:, :, :, C // 2 - 1:C // 2, :]

    q_in = qh * jnp.exp(bcum - ref)
    k_in = kh * jnp.exp(ref - bcum)
    causal = jnp.tril(jnp.ones((C, C), dtype=bool))
    att = jnp.where(causal, jnp.einsum('bhncd,bhnsd->bhncs', q_in, k_in), 0.0)
    o_intra = jnp.einsum('bhncs,bhnse->bhnce', att, vh)

    kv = jnp.einsum('bhncd,bhnce->bhnde', kh * jnp.exp(b_last - bcum), vh)
    dec = jnp.exp(b_last[:, :, :, 0, :])

    def step(S, inp):
        kv_n, dec_n = inp
        return dec_n[..., None] * S + kv_n, S

    S0 = jnp.zeros((B, H, Dh, Dh), jnp.float32)
    _, S_prev = lax.scan(step, S0, (kv.transpose(2, 0, 1, 3, 4), dec.transpose(2, 0, 1, 3)))
    S_prev = S_prev.transpose(1, 2, 0, 3, 4)
    o_inter = jnp.einsum('bhncd,bhnde->bhnce', qh * jnp.exp(bcum), S_prev)

    o = o_intra + o_inter
    o = o * lax.rsqrt(jnp.mean(o * o, axis=-1, keepdims=True) + EPS)
    o = o.transpose(0, 2, 3, 1, 4).reshape(B, T, GLA_WIDTH) * gla_gn.astype(jnp.float32)
    return (o * jax.nn.silu(g.astype(jnp.float32))).astype(q.dtype)


def fox_mixer(q, k, v, f_logit, b_f):
    B, T, _ = q.shape
    H, Dh = FOX_HEADS, FOX_HEAD_DIM
    nb = T // Q_BLOCK

    def heads(z):
        return z.astype(jnp.float32).reshape(B, T, H, Dh).transpose(0, 2, 1, 3)

    qh = heads(q) * (Dh ** -0.5)
    kh, vh = heads(k), heads(v)
    logf = jax.nn.log_sigmoid(f_logit.astype(jnp.float32) + b_f.astype(jnp.float32))
    F = jnp.cumsum(logf, axis=1).transpose(0, 2, 1)
    q_blocks = qh.reshape(B, H, nb, Q_BLOCK, Dh).transpose(2, 0, 1, 3, 4)
    F_blocks = F.reshape(B, H, nb, Q_BLOCK).transpose(2, 0, 1, 3)
    kpos = jnp.arange(T)

    def block(args):
        qb, Fb, i = args
        s = jnp.einsum('bhqd,bhkd->bhqk', qb, kh) + Fb[..., None] - F[:, :, None, :]
        qpos = i * Q_BLOCK + jnp.arange(Q_BLOCK)
        s = jnp.where(kpos[None, :] <= qpos[:, None], s, -jnp.inf)
        p = jax.nn.softmax(s, axis=-1)
        return jnp.einsum('bhqk,bhkd->bhqd', p, vh)

    o = lax.map(block, (q_blocks, F_blocks, jnp.arange(nb)))
    return o.transpose(1, 0, 3, 2, 4).reshape(B, T, FOX_WIDTH).astype(q.dtype)


def setup_inputs(seed: int = 0) -> dict:
    key = jax.random.key(seed)
    ks = jax.random.split(key, 16)
    f32 = jnp.float32
    nrm = lambda k, shape: jax.random.normal(k, shape, f32)
    return {
        "x": nrm(ks[0], (BATCH, SEQ, D_MODEL)),
        "ln1": 1.0 + 0.02 * nrm(ks[1], (DEPTH, D_MODEL)),
        "w_in": nrm(ks[2], (DEPTH, D_MODEL, IN_WIDTH)) * D_MODEL ** -0.5,
        "w_pool": nrm(ks[3], (DEPTH, POOL_GROUPS, POOL_GROUP, POOL_GROUP)) * POOL_GROUP ** -0.5,
        "pool_scale": 0.5 + 0.1 * nrm(ks[4], (DEPTH, POOL_WIDTH)),
        "w_a_up": nrm(ks[5], (DEPTH, GLA_GATE_RANK, GLA_WIDTH)) * GLA_GATE_RANK ** -0.5,
        "b_a": 0.1 * nrm(ks[6], (DEPTH, GLA_WIDTH)),
        "gla_gn": 1.0 + 0.02 * nrm(ks[7], (DEPTH, GLA_WIDTH)),
        "b_f": 3.0 + 0.5 * nrm(ks[8], (DEPTH, FOX_HEADS)),
        "w_o": nrm(ks[9], (DEPTH, D_MODEL, D_MODEL)) * D_MODEL ** -0.5,
        "ln2": 1.0 + 0.02 * nrm(ks[10], (DEPTH, D_MODEL)),
        "w_gu": nrm(ks[11], (DEPTH, D_MODEL, 2 * D_FF)) * D_MODEL ** -0.5,
        "w_down": nrm(ks[12], (DEPTH, D_FF, D_MODEL)) * D_FF ** -0.5,
        "ln_f": 1.0 + 0.02 * nrm(ks[13], (D_MODEL,)),
    }


def reference(x, ln1, w_in, w_pool, pool_scale, w_a_up, b_a, gla_gn, b_f, w_o, ln2, w_gu, w_down, ln_f):
    pts = _split_points()
    for l in range(DEPTH):
        h = rmsnorm(x, ln1[l])
        z = jnp.einsum('btd,de->bte', h, w_in[l])
        u_pool, gq, gk, gv, gg, ga, fq, fk, fv, ff = jnp.split(z, pts, axis=-1)
        y_pool = pool_mixer(u_pool, w_pool[l], pool_scale[l])
        y_gla = gla_mixer(gq, gk, gv, gg, ga, w_a_up[l], b_a[l], gla_gn[l])
        y_fox = fox_mixer(fq, fk, fv, ff, b_f[l])
        mix = jnp.concatenate([y_pool.astype(x.dtype), y_gla.astype(x.dtype), y_fox.astype(x.dtype)], axis=-1)
        x = x + jnp.einsum('btd,de->bte', mix, w_o[l])
        h = rmsnorm(x, ln2[l])
        gate, up = jnp.split(jnp.einsum('btd,df->btf', h, w_gu[l]), 2, axis=-1)
        x = x + jnp.einsum('btf,fd->btd', jax.nn.silu(gate) * up, w_down[l])
    return rmsnorm(x, ln_f)
```

```python
import functools

import jax
import jax.numpy as jnp
from jax import lax
from jax.experimental import pallas as pl
from jax.experimental.pallas import tpu as pltpu

D_MODEL = 1024
CHUNK = 64
EPS = 1e-6
POOL_WIDTH = 256
POOL_WINDOWS = (2, 4, 8, 16)
POOL_GROUP = 64
GLA_WIDTH = 256
GLA_HEADS = 4
GLA_HEAD_DIM = 64
GLA_GATE_RANK = 16
GLA_TAU = 16.0
FOX_WIDTH = 512
FOX_HEAD_DIM = 64
FOX_HEADS = 8
D_FF = 2816

LANES = 128
HALO = 16
FF_LANE0 = GLA_GATE_RANK
F_SLOT = 8
NEG_BIG = -1e30

VMEM_LIMIT = 56 * 1024 * 1024

TM_IN = 512
TM_MIX = 256
TQ = 256
TK = 256
TM_FFN = 256


def _dot(a, b):
    return jnp.dot(a, b, preferred_element_type=jnp.float32)


def _dot_nt(a, b):
    return lax.dot_general(a, b, (((1,), (1,)), ((), ())), preferred_element_type=jnp.float32)


def _dot_tn(a, b):
    return lax.dot_general(a, b, (((0,), (0,)), ((), ())), preferred_element_type=jnp.float32)


def _split3(x):
    hi = x.astype(jnp.bfloat16)
    r = x - hi.astype(jnp.float32)
    mid = r.astype(jnp.bfloat16)
    lo = (r - mid.astype(jnp.float32)).astype(jnp.bfloat16)
    return hi, mid, lo


def _log_sigmoid(x):
    return jnp.minimum(x, 0.0) - jnp.log(1.0 + jnp.exp(-jnp.abs(x)))


def _const_spec(shape):
    nd = len(shape)
    return pl.BlockSpec(shape, lambda *_: (0,) * nd)


def _inproj_kernel(x_ref, ln_ref, wu_ref, wg_ref, ws_ref, wq_ref, wk_ref, wvt_ref,
                   u_ref, gla_ref, small_ref, q_ref, k_ref, vt_ref):
    x = x_ref[...]
    ms = jnp.mean(x * x, axis=-1, keepdims=True)
    h = (x * lax.rsqrt(ms + EPS) * ln_ref[...]).astype(jnp.bfloat16)
    u_ref[...] = _dot(h, wu_ref[...])
    gla_ref[...] = _dot(h, wg_ref[...]).astype(jnp.bfloat16)
    small_ref[...] = _dot(h, ws_ref[...])
    q_ref[...] = _dot(h, wq_ref[...]).astype(jnp.bfloat16)
    k_ref[...] = _dot(h, wk_ref[...]).astype(jnp.bfloat16)
    vt_ref[...] = _dot_nt(wvt_ref[...], h).astype(jnp.bfloat16)


def _inproj(x2, ln, wu, wg, ws, wq, wk, wvt, batch, seq):
    bt = x2.shape[0]
    tm = TM_IN
    nt = seq // tm
    row = lambda w: pl.BlockSpec((tm, w), lambda i: (i, 0))
    return pl.pallas_call(
        _inproj_kernel,
        grid=(bt // tm,),
        in_specs=[row(D_MODEL), _const_spec((1, D_MODEL)),
                  _const_spec(wu.shape), _const_spec(wg.shape), _const_spec(ws.shape),
                  _const_spec(wq.shape), _const_spec(wk.shape), _const_spec(wvt.shape)],
        out_specs=[row(POOL_WIDTH), row(4 * GLA_WIDTH), row(LANES), row(FOX_WIDTH), row(FOX_WIDTH),
                   pl.BlockSpec((None, FOX_WIDTH, tm), lambda i: (i // nt, 0, i % nt))],
        out_shape=[jax.ShapeDtypeStruct((bt, POOL_WIDTH), jnp.float32),
                   jax.ShapeDtypeStruct((bt, 4 * GLA_WIDTH), jnp.bfloat16),
                   jax.ShapeDtypeStruct((bt, LANES), jnp.float32),
                   jax.ShapeDtypeStruct((bt, FOX_WIDTH), jnp.bfloat16),
                   jax.ShapeDtypeStruct((bt, FOX_WIDTH), jnp.bfloat16),
                   jax.ShapeDtypeStruct((batch, FOX_WIDTH, seq), jnp.bfloat16)],
        compiler_params=pltpu.CompilerParams(dimension_semantics=("arbitrary",),
                                             vmem_limit_bytes=VMEM_LIMIT),
        name="inproj",
    )(x2, ln, wu, wg, ws, wq, wk, wvt)


def _mixpg_kernel(u_ref, gla_ref, small_ref,
                  wpool_ref, pscale_ref, wa_hi_ref, wa_lo_ref, ba_ref, gn_ref, bf_ref,
                  tri_chunk_ref, tri_full_ref, pq_ref, cq_ref, pk_ref, ck_ref,
                  mpg_ref, fq_ref, fk_ref,
                  ubuf, fcarry, st_ref, bcum_ref, og_ref):
    ti = pl.program_id(1)
    tm = TM_MIX

    @pl.when(ti == 0)
    def _():
        ubuf[0:HALO, :] = jnp.zeros((HALO, POOL_WIDTH), jnp.float32)
        fcarry[...] = jnp.zeros_like(fcarry)
        st_ref[...] = jnp.zeros_like(st_ref)

    u = u_ref[...]
    ubuf[HALO:HALO + tm, :] = u
    acc = u
    sums = {}
    for k in range(1, POOL_WINDOWS[-1]):
        acc = acc + ubuf[HALO - k:HALO - k + tm, :]
        if k + 1 in POOL_WINDOWS:
            sums[k + 1] = acc
    lane = lax.broadcasted_iota(jnp.int32, (tm, POOL_WIDTH), 1)
    tpos = ti * tm + lax.broadcasted_iota(jnp.int32, (tm, POOL_WIDTH), 0)
    wsum = sums[POOL_WINDOWS[-1]]
    width = jnp.full((tm, POOL_WIDTH), POOL_WINDOWS[-1], jnp.int32)
    for gi in range(len(POOL_WINDOWS) - 2, -1, -1):
        in_group = lane < (gi + 1) * POOL_GROUP
        wsum = jnp.where(in_group, sums[POOL_WINDOWS[gi]], wsum)
        width = jnp.where(in_group, POOL_WINDOWS[gi], width)
    cnt = jnp.minimum(tpos + 1, width).astype(jnp.float32)
    d = wsum / cnt - u
    y_pool = _dot(d.astype(jnp.bfloat16), wpool_ref[...]) * pscale_ref[...]
    mpg_ref[:, 0:POOL_WIDTH] = y_pool.astype(jnp.bfloat16)
    ubuf[0:HALO, :] = ubuf[tm:tm + HALO, :]

    small = small_ref[...]
    s_hi = small.astype(jnp.bfloat16)
    s_lo = (small - s_hi.astype(jnp.float32)).astype(jnp.bfloat16)
    a = (_dot(s_hi, wa_hi_ref[...]) + _dot(s_lo, wa_hi_ref[...]) + _dot(s_hi, wa_lo_ref[...])
         + ba_ref[...])
    la = _log_sigmoid(a) / GLA_TAU
    la_hi, la_mid, la_lo = _split3(la)
    tri_c = tri_chunk_ref[...]
    bcum_ref[...] = _dot(tri_c, la_hi) + _dot(tri_c, la_mid) + _dot(tri_c, la_lo)

    slane = lax.broadcasted_iota(jnp.int32, (tm, LANES), 1)
    is_ff = (slane >= FF_LANE0) & (slane < FF_LANE0 + FOX_HEADS)
    lf = jnp.where(is_ff, _log_sigmoid(small + bf_ref[...]), 0.0)
    lf_hi, lf_mid, lf_lo = _split3(lf)
    tri_f = tri_full_ref[...]
    fcum = _dot(tri_f, lf_hi) + _dot(tri_f, lf_mid) + _dot(tri_f, lf_lo) + fcarry[...]
    fcarry[...] = fcum[tm - 1:tm, :]
    f_hi, f_mid, f_lo = _split3(fcum)
    fq_ref[...] = (_dot(f_hi, pq_ref[0]) + _dot(f_mid, pq_ref[1]) + _dot(f_lo, pq_ref[2])
                   + cq_ref[...]).astype(jnp.bfloat16)
    fk_ref[...] = (_dot(f_hi, pk_ref[0]) + _dot(f_mid, pk_ref[1]) + _dot(f_lo, pk_ref[2])
                   + ck_ref[...]).astype(jnp.bfloat16)

    glane = lax.broadcasted_iota(jnp.int32, (1, GLA_WIDTH), 1)
    head_masks = [(glane >= h * GLA_HEAD_DIM) & (glane < (h + 1) * GLA_HEAD_DIM)
                  for h in range(GLA_HEADS)]
    r_i = lax.broadcasted_iota(jnp.int32, (GLA_WIDTH, GLA_WIDTH), 0) // GLA_HEAD_DIM
    c_i = lax.broadcasted_iota(jnp.int32, (GLA_WIDTH, GLA_WIDTH), 1) // GLA_HEAD_DIM
    same_head = r_i == c_i
    causal = (lax.broadcasted_iota(jnp.int32, (CHUNK, CHUNK), 0)
              >= lax.broadcasted_iota(jnp.int32, (CHUNK, CHUNK), 1))

    for c in range(tm // CHUNK):
        r0 = c * CHUNK
        q = gla_ref[r0:r0 + CHUNK, 0:GLA_WIDTH].astype(jnp.float32)
        k = gla_ref[r0:r0 + CHUNK, GLA_WIDTH:2 * GLA_WIDTH].astype(jnp.float32)
        v = gla_ref[r0:r0 + CHUNK, 2 * GLA_WIDTH:3 * GLA_WIDTH]
        bc = bcum_ref[r0:r0 + CHUNK, :]
        b_last = bcum_ref[r0 + CHUNK - 1:r0 + CHUNK, :]
        b_mid = bcum_ref[r0 + CHUNK // 2 - 1:r0 + CHUNK // 2, :]
        q_in = (q * jnp.exp(bc - b_mid)).astype(jnp.bfloat16)
        k_in = (k * jnp.exp(b_mid - bc)).astype(jnp.bfloat16)
        k_kv = (k * jnp.exp(b_last - bc)).astype(jnp.bfloat16)
        q_st = (q * jnp.exp(bc)).astype(jnp.bfloat16)
        dec = jnp.exp(b_last)

        st = st_ref[...]
        o = _dot_nt(q_st, st.astype(jnp.bfloat16))
        for h in range(GLA_HEADS):
            att = _dot_nt(jnp.where(head_masks[h], q_in, jnp.zeros_like(q_in)), k_in)
            att = jnp.where(causal, att, 0.0).astype(jnp.bfloat16)
            o = o + jnp.where(head_masks[h], _dot(att, v), 0.0)
        kv_t = _dot_tn(v, k_kv)
        st_ref[...] = st * dec + jnp.where(same_head, kv_t, 0.0)
        og_ref[r0:r0 + CHUNK, :] = o

    o = og_ref[...]
    o2 = o * o
    mean_sq = jnp.zeros_like(o)
    for h in range(GLA_HEADS):
        hs = jnp.sum(jnp.where(head_masks[h], o2, 0.0), axis=-1, keepdims=True) / GLA_HEAD_DIM
        mean_sq = jnp.where(head_masks[h], hs, mean_sq)
    g = gla_ref[:, 3 * GLA_WIDTH:4 * GLA_WIDTH].astype(jnp.float32)
    y_gla = o * lax.rsqrt(mean_sq + EPS) * gn_ref[...] * (g * jax.nn.sigmoid(g))
    mpg_ref[:, POOL_WIDTH:POOL_WIDTH + GLA_WIDTH] = y_gla.astype(jnp.bfloat16)


def _mixpg(u, gla, small, consts, batch, seq):
    tm = TM_MIX
    nt = seq // tm
    row = lambda w: pl.BlockSpec((tm, w), lambda b, i: (b * nt + i, 0))
    bt = u.shape[0]
    return pl.pallas_call(
        _mixpg_kernel,
        grid=(batch, nt),
        in_specs=[row(POOL_WIDTH), row(4 * GLA_WIDTH), row(LANES)] + [_const_spec(c.shape) for c in consts],
        out_specs=[row(POOL_WIDTH + GLA_WIDTH), row(FOX_WIDTH), row(FOX_WIDTH)],
        out_shape=[jax.ShapeDtypeStruct((bt, POOL_WIDTH + GLA_WIDTH), jnp.bfloat16),
                   jax.ShapeDtypeStruct((bt, FOX_WIDTH), jnp.bfloat16),
                   jax.ShapeDtypeStruct((bt, FOX_WIDTH), jnp.bfloat16)],
        scratch_shapes=[pltpu.VMEM((tm + HALO, POOL_WIDTH), jnp.float32),
                        pltpu.VMEM((1, LANES), jnp.float32),
                        pltpu.VMEM((GLA_WIDTH, GLA_WIDTH), jnp.float32),
                        pltpu.VMEM((tm, GLA_WIDTH), jnp.float32),
                        pltpu.VMEM((tm, GLA_WIDTH), jnp.float32)],
        compiler_params=pltpu.CompilerParams(dimension_semantics=("arbitrary", "arbitrary"),
                                             vmem_limit_bytes=VMEM_LIMIT),
        name="mixpg",
    )(u, gla, small, *consts)


def _fox_kernel(q_ref, fq_ref, k_ref, fk_ref, vt_ref, o_ref):
    qi = pl.program_id(2)
    qext = jnp.concatenate([q_ref[...], fq_ref[...]], axis=1)
    lane = lax.broadcasted_iota(jnp.int32, (1, 2 * LANES), 1)
    ones_rows = jnp.ones((16, TK), jnp.bfloat16)

    def head_q(h):
        lo = h * FOX_HEAD_DIM
        flo = LANES + h * F_SLOT
        m = ((lane >= lo) & (lane < lo + FOX_HEAD_DIM)) | ((lane >= flo) & (lane < flo + F_SLOT))
        return jnp.where(m, qext, jnp.zeros_like(qext))

    qh = [head_q(0), head_q(1)]

    def block(j, carry, diagonal):
        k0 = pl.multiple_of(j * TK, TK)
        kext = jnp.concatenate([k_ref[pl.ds(k0, TK), :], fk_ref[pl.ds(k0, TK), :]], axis=1)
        vt = vt_ref[:, pl.ds(k0, TK)]
        out = []
        for h in range(2):
            m, acc = carry[h]
            s = _dot_nt(kext, qh[h])
            if diagonal:
                krow = lax.broadcasted_iota(jnp.int32, (TK, TQ), 0)
                qcol = lax.broadcasted_iota(jnp.int32, (TK, TQ), 1)
                s = jnp.where(krow <= qcol, s, NEG_BIG)
            m_new = jnp.maximum(m, jnp.max(s, axis=0, keepdims=True))
            alpha = jnp.exp(m - m_new)
            p = jnp.exp(s - m_new).astype(jnp.bfloat16)
            vth = jnp.concatenate([vt[h * FOX_HEAD_DIM:(h + 1) * FOX_HEAD_DIM, :], ones_rows], axis=0)
            acc = alpha * acc + _dot(vth, p)
            out.append((m_new, acc))
        return tuple(out)

    init = tuple((jnp.full((1, TQ), NEG_BIG, jnp.float32),
                  jnp.zeros((FOX_HEAD_DIM + 16, TQ), jnp.float32)) for _ in range(2))
    carry = lax.fori_loop(0, qi, lambda j, c: block(j, c, False), init)
    carry = block(qi, carry, True)
    o_t = jnp.concatenate([acc[0:FOX_HEAD_DIM, :] / acc[FOX_HEAD_DIM:FOX_HEAD_DIM + 1, :]
                           for _, acc in carry], axis=0)
    o_ref[...] = o_t.T.astype(jnp.bfloat16)


def _fox(q, fq, k, fk, vt, batch, seq):
    nq = seq // TQ
    pairs = FOX_HEADS // 2
    qspec = pl.BlockSpec((TQ, LANES), lambda b, p, i: (b * nq + i, p))
    kspec = pl.BlockSpec((seq, LANES), lambda b, p, i: (b, p))
    return pl.pallas_call(
        _fox_kernel,
        grid=(batch, pairs, nq),
        in_specs=[qspec, qspec, kspec, kspec,
                  pl.BlockSpec((None, LANES, seq), lambda b, p, i: (b, p, 0))],
        out_specs=qspec,
        out_shape=jax.ShapeDtypeStruct(q.shape, jnp.bfloat16),
        compiler_params=pltpu.CompilerParams(
            dimension_semantics=("arbitrary", "arbitrary", "arbitrary"),
            vmem_limit_bytes=VMEM_LIMIT),
        name="fox",
    )(q, fq, k, fk, vt)


def _outffn_kernel(x_ref, mpg_ref, yf_ref, wo_ref, ln2_ref, wgu_ref, wdn_ref, lnf_ref, o_ref, *, final):
    half = POOL_WIDTH + GLA_WIDTH
    x1 = (x_ref[...] + _dot(mpg_ref[...], wo_ref[0:half, :]) + _dot(yf_ref[...], wo_ref[half:, :]))
    ms = jnp.mean(x1 * x1, axis=-1, keepdims=True)
    h = (x1 * lax.rsqrt(ms + EPS) * ln2_ref[...]).astype(jnp.bfloat16)
    gate = _dot(h, wgu_ref[:, 0:D_FF])
    up = _dot(h, wgu_ref[:, D_FF:])
    act = (gate * jax.nn.sigmoid(gate) * up).astype(jnp.bfloat16)
    x2 = x1 + _dot(act, wdn_ref[...])
    if final:
        ms2 = jnp.mean(x2 * x2, axis=-1, keepdims=True)
        x2 = x2 * lax.rsqrt(ms2 + EPS) * lnf_ref[...]
    o_ref[...] = x2


def _outffn(x2, mpg, yf, wo, ln2, wgu, wdn, lnf, final):
    bt = x2.shape[0]
    tm = TM_FFN
    row = lambda w: pl.BlockSpec((tm, w), lambda i: (i, 0))
    single = lambda shape: pl.BlockSpec(shape, lambda i: (0, 0), pipeline_mode=pl.Buffered(1))
    return pl.pallas_call(
        functools.partial(_outffn_kernel, final=final),
        grid=(bt // tm,),
        in_specs=[row(D_MODEL), row(POOL_WIDTH + GLA_WIDTH), row(FOX_WIDTH),
                  single(wo.shape), _const_spec((1, D_MODEL)), single(wgu.shape), single(wdn.shape),
                  _const_spec((1, D_MODEL))],
        out_specs=row(D_MODEL),
        out_shape=jax.ShapeDtypeStruct((bt, D_MODEL), jnp.float32),
        compiler_params=pltpu.CompilerParams(dimension_semantics=("arbitrary",),
                                             vmem_limit_bytes=VMEM_LIMIT),
        name="outffn",
    )(x2, mpg, yf, wo, ln2, wgu, wdn, lnf)


def _placement(dtype=jnp.bfloat16):
    pq = jnp.zeros((3, LANES, FOX_WIDTH), jnp.float32)
    pk = jnp.zeros((3, LANES, FOX_WIDTH), jnp.float32)
    cq = jnp.zeros((1, FOX_WIDTH), jnp.float32)
    ck = jnp.zeros((1, FOX_WIDTH), jnp.float32)
    for h in range(FOX_HEADS):
        base = (h // 2) * LANES + (h % 2) * F_SLOT
        for part in range(3):
            pq = pq.at[part, FF_LANE0 + h, base + part].set(1.0)
            pk = pk.at[part, FF_LANE0 + h, base + 3 + part].set(-1.0)
            cq = cq.at[0, base + 3 + part].set(1.0)
            ck = ck.at[0, base + part].set(1.0)
    return pq.astype(dtype), cq, pk.astype(dtype), ck


@jax.jit
def kernel(x, ln1, w_in, w_pool, pool_scale, w_a_up, b_a, gla_gn, b_f, w_o, ln2, w_gu, w_down, ln_f):
    batch, seq, _ = x.shape
    depth = w_in.shape[0]
    bf16 = jnp.bfloat16
    xf = x.reshape(batch * seq, D_MODEL)

    r = lax.broadcasted_iota(jnp.int32, (TM_MIX, TM_MIX), 0)
    c = lax.broadcasted_iota(jnp.int32, (TM_MIX, TM_MIX), 1)
    tri_full = (r >= c).astype(bf16)
    tri_chunk = ((r >= c) & (r // CHUNK == c // CHUNK)).astype(bf16)
    pq, cq, pk, ck = _placement()

    o0 = POOL_WIDTH
    o1 = o0 + 4 * GLA_WIDTH
    o2 = o1 + GLA_GATE_RANK
    o3 = o2 + 3 * FOX_WIDTH
    for l in range(depth):
        w = w_in[l]
        gla_scale = jnp.concatenate([jnp.full((GLA_WIDTH,), GLA_HEAD_DIM ** -0.5, jnp.float32),
                                     jnp.ones((3 * GLA_WIDTH,), jnp.float32)])
        wu = w[:, 0:o0].astype(bf16)
        wg = (w[:, o0:o1] * gla_scale).astype(bf16)
        ws = jnp.concatenate([w[:, o1:o2], w[:, o3:o3 + FOX_HEADS],
                              jnp.zeros((D_MODEL, LANES - GLA_GATE_RANK - FOX_HEADS), jnp.float32)],
                             axis=1).astype(bf16)
        wq = (w[:, o2:o2 + FOX_WIDTH] * FOX_HEAD_DIM ** -0.5).astype(bf16)
        wk = w[:, o2 + FOX_WIDTH:o2 + 2 * FOX_WIDTH].astype(bf16)
        wvt = w[:, o2 + 2 * FOX_WIDTH:o3].T.astype(bf16)
        u, gla, small, q, k, vt = _inproj(xf, ln1[l][None, :], wu, wg, ws, wq, wk, wvt, batch, seq)

        wpool_bd = jnp.zeros((POOL_WIDTH, POOL_WIDTH), jnp.float32)
        for g in range(len(POOL_WINDOWS)):
            sl = slice(g * POOL_GROUP, (g + 1) * POOL_GROUP)
            wpool_bd = wpool_bd.at[sl, sl].set(w_pool[l, g])
        wa = jnp.zeros((LANES, GLA_WIDTH), jnp.float32).at[0:GLA_GATE_RANK].set(w_a_up[l])
        wa_hi = wa.astype(bf16)
        wa_lo = (wa - wa_hi.astype(jnp.float32)).astype(bf16)
        bf_row = jnp.zeros((1, LANES), jnp.float32).at[0, FF_LANE0:FF_LANE0 + FOX_HEADS].set(b_f[l])
        consts = [wpool_bd.astype(bf16), pool_scale[l][None, :], wa_hi, wa_lo, b_a[l][None, :],
                  gla_gn[l][None, :], bf_row, tri_chunk, tri_full, pq, cq, pk, ck]
        mpg, fq, fk = _mixpg(u, gla, small, consts, batch, seq)

        yf = _fox(q, fq, k, fk, vt, batch, seq)

        xf = _outffn(xf, mpg, yf, w_o[l].astype(bf16), ln2[l][None, :], w_gu[l].astype(bf16),
                     w_down[l].astype(bf16), ln_f[None, :], final=(l == depth - 1))
    return xf.reshape(batch, seq, D_MODEL)
```

```python
import functools

import jax
import jax.numpy as jnp
from jax import lax
from jax.experimental import pallas as pl
from jax.experimental.pallas import tpu as pltpu

D_MODEL = 1024
CHUNK = 64
EPS = 1e-6
POOL_WIDTH = 256
POOL_WINDOWS = (2, 4, 8, 16)
POOL_GROUP = 64
GLA_WIDTH = 256
GLA_HEADS = 4
GLA_HEAD_DIM = 64
GLA_GATE_RANK = 16
GLA_TAU = 16.0
FOX_WIDTH = 512
FOX_HEAD_DIM = 64
FOX_HEADS = 8
D_FF = 2816

LANES = 128
HALO = 16
FF_LANE0 = GLA_GATE_RANK
F_SLOT = 8
NEG_BIG = -1e30

VMEM_LIMIT = 56 * 1024 * 1024

TM_IN = 512
TM_MIX = 256
TQ = 512
TM_FFN = 256


def _dot(a, b):
    return jnp.dot(a, b, preferred_element_type=jnp.float32)


def _dot_nt(a, b):
    return lax.dot_general(a, b, (((1,), (1,)), ((), ())), preferred_element_type=jnp.float32)


def _dot_tn(a, b):
    return lax.dot_general(a, b, (((0,), (0,)), ((), ())), preferred_element_type=jnp.float32)


def _split3(x):
    hi = x.astype(jnp.bfloat16)
    r = x - hi.astype(jnp.float32)
    mid = r.astype(jnp.bfloat16)
    lo = (r - mid.astype(jnp.float32)).astype(jnp.bfloat16)
    return hi, mid, lo


def _log_sigmoid(x):
    return jnp.minimum(x, 0.0) - jnp.log(1.0 + jnp.exp(-jnp.abs(x)))


def _const_spec(shape):
    nd = len(shape)
    return pl.BlockSpec(shape, lambda *_: (0,) * nd)


def _inproj_kernel(x_ref, ln_ref, wu_ref, wg_ref, ws_ref, wq_ref, wk_ref, wvt_ref,
                   u_ref, gla_ref, small_ref, q_ref, k_ref, vt_ref):
    x = x_ref[...]
    ms = jnp.mean(x * x, axis=-1, keepdims=True)
    h = (x * lax.rsqrt(ms + EPS) * ln_ref[...]).astype(jnp.bfloat16)
    u_ref[...] = _dot(h, wu_ref[...])
    gla_ref[...] = _dot(h, wg_ref[...]).astype(jnp.bfloat16)
    small_ref[...] = _dot(h, ws_ref[...])
    q_ref[...] = _dot(h, wq_ref[...]).astype(jnp.bfloat16)
    k_ref[...] = _dot(h, wk_ref[...]).astype(jnp.bfloat16)
    vt_ref[...] = _dot_nt(wvt_ref[...], h).astype(jnp.bfloat16)


def _inproj(x2, ln, wu, wg, ws, wq, wk, wvt, batch, seq):
    bt = x2.shape[0]
    tm = TM_IN
    nt = seq // tm
    row = lambda w: pl.BlockSpec((tm, w), lambda i: (i, 0))
    return pl.pallas_call(
        _inproj_kernel,
        grid=(bt // tm,),
        in_specs=[row(D_MODEL), _const_spec((1, D_MODEL)),
                  _const_spec(wu.shape), _const_spec(wg.shape), _const_spec(ws.shape),
                  _const_spec(wq.shape), _const_spec(wk.shape), _const_spec(wvt.shape)],
        out_specs=[row(POOL_WIDTH), row(4 * GLA_WIDTH), row(LANES), row(FOX_WIDTH), row(FOX_WIDTH),
                   pl.BlockSpec((None, FOX_WIDTH, tm), lambda i: (i // nt, 0, i % nt))],
        out_shape=[jax.ShapeDtypeStruct((bt, POOL_WIDTH), jnp.float32),
                   jax.ShapeDtypeStruct((bt, 4 * GLA_WIDTH), jnp.bfloat16),
                   jax.ShapeDtypeStruct((bt, LANES), jnp.float32),
                   jax.ShapeDtypeStruct((bt, FOX_WIDTH), jnp.bfloat16),
                   jax.ShapeDtypeStruct((bt, FOX_WIDTH), jnp.bfloat16),
                   jax.ShapeDtypeStruct((batch, FOX_WIDTH, seq), jnp.bfloat16)],
        compiler_params=pltpu.CompilerParams(dimension_semantics=("arbitrary",),
                                             vmem_limit_bytes=VMEM_LIMIT),
        name="inproj",
    )(x2, ln, wu, wg, ws, wq, wk, wvt)


def _mixpg_kernel(u_ref, gla_ref, small_ref,
                  wpool_ref, pscale_ref, wa_hi_ref, wa_lo_ref, ba_ref, gn_ref, bf_ref,
                  tri_chunk_ref, tri_full_ref, pq_ref, cq_ref, pk_ref, ck_ref,
                  mpg_ref, fq_ref, fk_ref,
                  ubuf, fcarry, st_ref, bcum_ref, og_ref):
    ti = pl.program_id(1)
    tm = TM_MIX

    @pl.when(ti == 0)
    def _():
        ubuf[0:HALO, :] = jnp.zeros((HALO, POOL_WIDTH), jnp.float32)
        fcarry[...] = jnp.zeros_like(fcarry)
        st_ref[...] = jnp.zeros_like(st_ref)

    u = u_ref[...]
    ubuf[HALO:HALO + tm, :] = u
    acc = u
    sums = {}
    for k in range(1, POOL_WINDOWS[-1]):
        acc = acc + ubuf[HALO - k:HALO - k + tm, :]
        if k + 1 in POOL_WINDOWS:
            sums[k + 1] = acc
    lane = lax.broadcasted_iota(jnp.int32, (tm, POOL_WIDTH), 1)
    tpos = ti * tm + lax.broadcasted_iota(jnp.int32, (tm, POOL_WIDTH), 0)
    wsum = sums[POOL_WINDOWS[-1]]
    width = jnp.full((tm, POOL_WIDTH), POOL_WINDOWS[-1], jnp.int32)
    for gi in range(len(POOL_WINDOWS) - 2, -1, -1):
        in_group = lane < (gi + 1) * POOL_GROUP
        wsum = jnp.where(in_group, sums[POOL_WINDOWS[gi]], wsum)
        width = jnp.where(in_group, POOL_WINDOWS[gi], width)
    cnt = jnp.minimum(tpos + 1, width).astype(jnp.float32)
    d = wsum / cnt - u
    y_pool = _dot(d.astype(jnp.bfloat16), wpool_ref[...]) * pscale_ref[...]
    mpg_ref[:, 0:POOL_WIDTH] = y_pool.astype(jnp.bfloat16)
    ubuf[0:HALO, :] = ubuf[tm:tm + HALO, :]

    small = small_ref[...]
    s_hi = small.astype(jnp.bfloat16)
    s_lo = (small - s_hi.astype(jnp.float32)).astype(jnp.bfloat16)
    a = (_dot(s_hi, wa_hi_ref[...]) + _dot(s_lo, wa_hi_ref[...]) + _dot(s_hi, wa_lo_ref[...])
         + ba_ref[...])
    la = _log_sigmoid(a) / GLA_TAU
    la_hi, la_mid, la_lo = _split3(la)
    tri_c = tri_chunk_ref[...]
    bcum_ref[...] = _dot(tri_c, la_hi) + _dot(tri_c, la_mid) + _dot(tri_c, la_lo)

    slane = lax.broadcasted_iota(jnp.int32, (tm, LANES), 1)
    is_ff = (slane >= FF_LANE0) & (slane < FF_LANE0 + FOX_HEADS)
    lf = jnp.where(is_ff, _log_sigmoid(small + bf_ref[...]), 0.0)
    lf_hi, lf_mid, lf_lo = _split3(lf)
    tri_f = tri_full_ref[...]
    fcum = _dot(tri_f, lf_hi) + _dot(tri_f, lf_mid) + _dot(tri_f, lf_lo) + fcarry[...]
    fcarry[...] = fcum[tm - 1:tm, :]
    f_hi, f_mid, f_lo = _split3(fcum)
    fq_ref[...] = (_dot(f_hi, pq_ref[0]) + _dot(f_mid, pq_ref[1]) + _dot(f_lo, pq_ref[2])
                   + cq_ref[...]).astype(jnp.bfloat16)
    fk_ref[...] = (_dot(f_hi, pk_ref[0]) + _dot(f_mid, pk_ref[1]) + _dot(f_lo, pk_ref[2])
                   + ck_ref[...]).astype(jnp.bfloat16)

    glane = lax.broadcasted_iota(jnp.int32, (1, GLA_WIDTH), 1)
    head_masks = [(glane >= h * GLA_HEAD_DIM) & (glane < (h + 1) * GLA_HEAD_DIM)
                  for h in range(GLA_HEADS)]
    r_i = lax.broadcasted_iota(jnp.int32, (GLA_WIDTH, GLA_WIDTH), 0) // GLA_HEAD_DIM
    c_i = lax.broadcasted_iota(jnp.int32, (GLA_WIDTH, GLA_WIDTH), 1) // GLA_HEAD_DIM
    same_head = r_i == c_i
    causal = (lax.broadcasted_iota(jnp.int32, (CHUNK, CHUNK), 0)
              >= lax.broadcasted_iota(jnp.int32, (CHUNK, CHUNK), 1))

    for c in range(tm // CHUNK):
        r0 = c * CHUNK
        q = gla_ref[r0:r0 + CHUNK, 0:GLA_WIDTH].astype(jnp.float32)
        k = gla_ref[r0:r0 + CHUNK, GLA_WIDTH:2 * GLA_WIDTH].astype(jnp.float32)
        v = gla_ref[r0:r0 + CHUNK, 2 * GLA_WIDTH:3 * GLA_WIDTH]
        bc = bcum_ref[r0:r0 + CHUNK, :]
        b_last = bcum_ref[r0 + CHUNK - 1:r0 + CHUNK, :]
        b_mid = bcum_ref[r0 + CHUNK // 2 - 1:r0 + CHUNK // 2, :]
        q_in = (q * jnp.exp(bc - b_mid)).astype(jnp.bfloat16)
        k_in = (k * jnp.exp(b_mid - bc)).astype(jnp.bfloat16)
        k_kv = (k * jnp.exp(b_last - bc)).astype(jnp.bfloat16)
        q_st = (q * jnp.exp(bc)).astype(jnp.bfloat16)
        dec = jnp.exp(b_last)

        st = st_ref[...]
        o = _dot_nt(q_st, st.astype(jnp.bfloat16))
        for h in range(GLA_HEADS):
            att = _dot_nt(jnp.where(head_masks[h], q_in, jnp.zeros_like(q_in)), k_in)
            att = jnp.where(causal, att, 0.0).astype(jnp.bfloat16)
            o = o + jnp.where(head_masks[h], _dot(att, v), 0.0)
        kv_t = _dot_tn(v, k_kv)
        st_ref[...] = st * dec + jnp.where(same_head, kv_t, 0.0)
        og_ref[r0:r0 + CHUNK, :] = o

    o = og_ref[...]
    o2 = o * o
    mean_sq = jnp.zeros_like(o)
    for h in range(GLA_HEADS):
        hs = jnp.sum(jnp.where(head_masks[h], o2, 0.0), axis=-1, keepdims=True) / GLA_HEAD_DIM
        mean_sq = jnp.where(head_masks[h], hs, mean_sq)
    g = gla_ref[:, 3 * GLA_WIDTH:4 * GLA_WIDTH].astype(jnp.float32)
    y_gla = o * lax.rsqrt(mean_sq + EPS) * gn_ref[...] * (g * jax.nn.sigmoid(g))
    mpg_ref[:, POOL_WIDTH:POOL_WIDTH + GLA_WIDTH] = y_gla.astype(jnp.bfloat16)


def _mixpg(u, gla, small, consts, batch, seq):
    tm = TM_MIX
    nt = seq // tm
    row = lambda w: pl.BlockSpec((tm, w), lambda b, i: (b * nt + i, 0))
    bt = u.shape[0]
    return pl.pallas_call(
        _mixpg_kernel,
        grid=(batch, nt),
        in_specs=[row(POOL_WIDTH), row(4 * GLA_WIDTH), row(LANES)] + [_const_spec(c.shape) for c in consts],
        out_specs=[row(POOL_WIDTH + GLA_WIDTH), row(FOX_WIDTH), row(FOX_WIDTH)],
        out_shape=[jax.ShapeDtypeStruct((bt, POOL_WIDTH + GLA_WIDTH), jnp.bfloat16),
                   jax.ShapeDtypeStruct((bt, FOX_WIDTH), jnp.bfloat16),
                   jax.ShapeDtypeStruct((bt, FOX_WIDTH), jnp.bfloat16)],
        scratch_shapes=[pltpu.VMEM((tm + HALO, POOL_WIDTH), jnp.float32),
                        pltpu.VMEM((1, LANES), jnp.float32),
                        pltpu.VMEM((GLA_WIDTH, GLA_WIDTH), jnp.float32),
                        pltpu.VMEM((tm, GLA_WIDTH), jnp.float32),
                        pltpu.VMEM((tm, GLA_WIDTH), jnp.float32)],
        compiler_params=pltpu.CompilerParams(dimension_semantics=("arbitrary", "arbitrary"),
                                             vmem_limit_bytes=VMEM_LIMIT),
        name="mixpg",
    )(u, gla, small, *consts)


def _fox_kernel(q_ref, fq_ref, k_ref, fk_ref, vt_ref, o_ref,
                qh0_ref, qh1_ref, s0_ref, s1_ref, bm_ref, m_ref, acc0_ref, acc1_ref):
    qi = pl.program_id(2)
    qh_refs = (qh0_ref, qh1_ref)
    s_refs = (s0_ref, s1_ref)
    acc_refs = (acc0_ref, acc1_ref)

    qext = jnp.concatenate([q_ref[...], fq_ref[...]], axis=1)
    lane = lax.broadcasted_iota(jnp.int32, (1, 2 * LANES), 1)
    for h in range(2):
        lo = h * FOX_HEAD_DIM
        flo = LANES + h * F_SLOT
        own = ((lane >= lo) & (lane < lo + FOX_HEAD_DIM)) | ((lane >= flo) & (lane < flo + F_SLOT))
        qh_refs[h][...] = jnp.where(own, qext, jnp.zeros_like(qext))
        acc_refs[h][...] = jnp.zeros_like(acc_refs[h])
    m_ref[...] = jnp.full(m_ref.shape, NEG_BIG, jnp.float32)

    def scores(j, h):
        k0 = pl.multiple_of(j * TQ, TQ)
        kext = jnp.concatenate([k_ref[pl.ds(k0, TQ), :], fk_ref[pl.ds(k0, TQ), :]], axis=1)
        s = _dot_nt(kext, qh_refs[h][...])
        s_refs[h][...] = s
        bm_ref[h] = jnp.max(s, axis=0, keepdims=True)

    def attend(j, h, diagonal):
        k0 = pl.multiple_of(j * TQ, TQ)
        s = s_refs[h][...]
        if diagonal:
            krow = lax.broadcasted_iota(jnp.int32, (TQ, TQ), 0)
            qcol = lax.broadcasted_iota(jnp.int32, (TQ, TQ), 1)
            s = jnp.where(krow <= qcol, s, NEG_BIG)
            bm = jnp.max(s, axis=0, keepdims=True)
        else:
            bm = bm_ref[h]
        m = m_ref[h]
        m_new = jnp.maximum(m, bm)
        alpha = jnp.exp(m - m_new)
        p = jnp.exp(s - m_new).astype(jnp.bfloat16)
        vth = jnp.concatenate([vt_ref[h * FOX_HEAD_DIM:(h + 1) * FOX_HEAD_DIM, pl.ds(k0, TQ)],
                               jnp.ones((16, TQ), jnp.bfloat16)], axis=0)
        acc_refs[h][...] = alpha * acc_refs[h][...] + _dot(vth, p)
        m_ref[h] = m_new

    scores(0, 0)

    def body(j, carry):
        scores(j, 1)
        attend(j, 0, False)
        scores(j + 1, 0)
        attend(j, 1, False)
        return carry

    lax.fori_loop(0, qi, body, 0)
    scores(qi, 1)
    attend(qi, 0, True)
    attend(qi, 1, True)
    o_t = jnp.concatenate([a[0:FOX_HEAD_DIM, :] / a[FOX_HEAD_DIM:FOX_HEAD_DIM + 1, :]
                           for a in (acc0_ref[...], acc1_ref[...])], axis=0)
    o_ref[...] = o_t.T.astype(jnp.bfloat16)


def _fox(q, fq, k, fk, vt, batch, seq):
    nq = seq // TQ
    pairs = FOX_HEADS // 2
    qspec = pl.BlockSpec((TQ, LANES), lambda b, p, i: (b * nq + i, p))
    kspec = pl.BlockSpec((seq, LANES), lambda b, p, i: (b, p))
    acc_rows = FOX_HEAD_DIM + 16
    return pl.pallas_call(
        _fox_kernel,
        grid=(batch, pairs, nq),
        in_specs=[qspec, qspec, kspec, kspec,
                  pl.BlockSpec((None, LANES, seq), lambda b, p, i: (b, p, 0))],
        out_specs=qspec,
        out_shape=jax.ShapeDtypeStruct(q.shape, jnp.bfloat16),
        scratch_shapes=[pltpu.VMEM((TQ, 2 * LANES), jnp.bfloat16),
                        pltpu.VMEM((TQ, 2 * LANES), jnp.bfloat16),
                        pltpu.VMEM((TQ, TQ), jnp.float32),
                        pltpu.VMEM((TQ, TQ), jnp.float32),
                        pltpu.VMEM((2, 1, TQ), jnp.float32),
                        pltpu.VMEM((2, 1, TQ), jnp.float32),
                        pltpu.VMEM((acc_rows, TQ), jnp.float32),
                        pltpu.VMEM((acc_rows, TQ), jnp.float32)],
        compiler_params=pltpu.CompilerParams(
            dimension_semantics=("arbitrary", "arbitrary", "arbitrary"),
            vmem_limit_bytes=VMEM_LIMIT),
        name="fox",
    )(q, fq, k, fk, vt)


def _outffn_kernel(x_ref, mpg_ref, yf_ref, wo_ref, ln2_ref, wgu_ref, wdn_ref, lnf_ref, o_ref, *, final):
    half = POOL_WIDTH + GLA_WIDTH
    x1 = (x_ref[...] + _dot(mpg_ref[...], wo_ref[0:half, :]) + _dot(yf_ref[...], wo_ref[half:, :]))
    ms = jnp.mean(x1 * x1, axis=-1, keepdims=True)
    h = (x1 * lax.rsqrt(ms + EPS) * ln2_ref[...]).astype(jnp.bfloat16)
    gate = _dot(h, wgu_ref[:, 0:D_FF])
    up = _dot(h, wgu_ref[:, D_FF:])
    act = (gate * jax.nn.sigmoid(gate) * up).astype(jnp.bfloat16)
    x2 = x1 + _dot(act, wdn_ref[...])
    if final:
        ms2 = jnp.mean(x2 * x2, axis=-1, keepdims=True)
        x2 = x2 * lax.rsqrt(ms2 + EPS) * lnf_ref[...]
    o_ref[...] = x2


def _outffn(x2, mpg, yf, wo, ln2, wgu, wdn, lnf, final):
    bt = x2.shape[0]
    tm = TM_FFN
    row = lambda w: pl.BlockSpec((tm, w), lambda i: (i, 0))
    single = lambda shape: pl.BlockSpec(shape, lambda i: (0, 0), pipeline_mode=pl.Buffered(1))
    return pl.pallas_call(
        functools.partial(_outffn_kernel, final=final),
        grid=(bt // tm,),
        in_specs=[row(D_MODEL), row(POOL_WIDTH + GLA_WIDTH), row(FOX_WIDTH),
                  single(wo.shape), _const_spec((1, D_MODEL)), single(wgu.shape), single(wdn.shape),
                  _const_spec((1, D_MODEL))],
        out_specs=row(D_MODEL),
        out_shape=jax.ShapeDtypeStruct((bt, D_MODEL), jnp.float32),
        compiler_params=pltpu.CompilerParams(dimension_semantics=("arbitrary",),
                                             vmem_limit_bytes=VMEM_LIMIT),
        name="outffn",
    )(x2, mpg, yf, wo, ln2, wgu, wdn, lnf)


def _placement(dtype=jnp.bfloat16):
    pq = jnp.zeros((3, LANES, FOX_WIDTH), jnp.float32)
    pk = jnp.zeros((3, LANES, FOX_WIDTH), jnp.float32)
    cq = jnp.zeros((1, FOX_WIDTH), jnp.float32)
    ck = jnp.zeros((1, FOX_WIDTH), jnp.float32)
    for h in range(FOX_HEADS):
        base = (h // 2) * LANES + (h % 2) * F_SLOT
        for part in range(3):
            pq = pq.at[part, FF_LANE0 + h, base + part].set(1.0)
            pk = pk.at[part, FF_LANE0 + h, base + 3 + part].set(-1.0)
            cq = cq.at[0, base + 3 + part].set(1.0)
            ck = ck.at[0, base + part].set(1.0)
    return pq.astype(dtype), cq, pk.astype(dtype), ck


@jax.jit
def kernel(x, ln1, w_in, w_pool, pool_scale, w_a_up, b_a, gla_gn, b_f, w_o, ln2, w_gu, w_down, ln_f):
    batch, seq, _ = x.shape
    depth = w_in.shape[0]
    bf16 = jnp.bfloat16
    xf = x.reshape(batch * seq, D_MODEL)

    r = lax.broadcasted_iota(jnp.int32, (TM_MIX, TM_MIX), 0)
    c = lax.broadcasted_iota(jnp.int32, (TM_MIX, TM_MIX), 1)
    tri_full = (r >= c).astype(bf16)
    tri_chunk = ((r >= c) & (r // CHUNK == c // CHUNK)).astype(bf16)
    pq, cq, pk, ck = _placement()

    o0 = POOL_WIDTH
    o1 = o0 + 4 * GLA_WIDTH
    o2 = o1 + GLA_GATE_RANK
    o3 = o2 + 3 * FOX_WIDTH
    for l in range(depth):
        w = w_in[l]
        gla_scale = jnp.concatenate([jnp.full((GLA_WIDTH,), GLA_HEAD_DIM ** -0.5, jnp.float32),
                                     jnp.ones((3 * GLA_WIDTH,), jnp.float32)])
        wu = w[:, 0:o0].astype(bf16)
        wg = (w[:, o0:o1] * gla_scale).astype(bf16)
        ws = jnp.concatenate([w[:, o1:o2], w[:, o3:o3 + FOX_HEADS],
                              jnp.zeros((D_MODEL, LANES - GLA_GATE_RANK - FOX_HEADS), jnp.float32)],
                             axis=1).astype(bf16)
        wq = (w[:, o2:o2 + FOX_WIDTH] * FOX_HEAD_DIM ** -0.5).astype(bf16)
        wk = w[:, o2 + FOX_WIDTH:o2 + 2 * FOX_WIDTH].astype(bf16)
        wvt = w[:, o2 + 2 * FOX_WIDTH:o3].T.astype(bf16)
        u, gla, small, q, k, vt = _inproj(xf, ln1[l][None, :], wu, wg, ws, wq, wk, wvt, batch, seq)

        wpool_bd = jnp.zeros((POOL_WIDTH, POOL_WIDTH), jnp.float32)
        for g in range(len(POOL_WINDOWS)):
            sl = slice(g * POOL_GROUP, (g + 1) * POOL_GROUP)
            wpool_bd = wpool_bd.at[sl, sl].set(w_pool[l, g])
        wa = jnp.zeros((LANES, GLA_WIDTH), jnp.float32).at[0:GLA_GATE_RANK].set(w_a_up[l])
        wa_hi = wa.astype(bf16)
        wa_lo = (wa - wa_hi.astype(jnp.float32)).astype(bf16)
        bf_row = jnp.zeros((1, LANES), jnp.float32).at[0, FF_LANE0:FF_LANE0 + FOX_HEADS].set(b_f[l])
        consts = [wpool_bd.astype(bf16), pool_scale[l][None, :], wa_hi, wa_lo, b_a[l][None, :],
                  gla_gn[l][None, :], bf_row, tri_chunk, tri_full, pq, cq, pk, ck]
        mpg, fq, fk = _mixpg(u, gla, small, consts, batch, seq)

        yf = _fox(q, fq, k, fk, vt, batch, seq)

        xf = _outffn(xf, mpg, yf, w_o[l].astype(bf16), ln2[l][None, :], w_gu[l].astype(bf16),
                     w_down[l].astype(bf16), ln_f[None, :], final=(l == depth - 1))
    return xf.reshape(batch, seq, D_MODEL)
```

```python
import functools

import jax
import jax.numpy as jnp
import numpy as np
from jax import lax
from jax.experimental import pallas as pl
from jax.experimental.pallas import tpu as pltpu

D_MODEL = 1024
CHUNK = 64
EPS = 1e-6
POOL_WIDTH = 256
POOL_WINDOWS = (2, 4, 8, 16)
POOL_GROUP = 64
GLA_WIDTH = 256
GLA_HEADS = 4
GLA_HEAD_DIM = 64
GLA_GATE_RANK = 16
GLA_TAU = 16.0
FOX_WIDTH = 512
FOX_HEAD_DIM = 64
FOX_HEADS = 8
D_FF = 2816

LANES = 128
HALO = 16
FF_LANE0 = GLA_GATE_RANK
F_SLOT = 8
NEG_BIG = -1e30
LOG2E = 1.4426950408889634

VMEM_LIMIT = 56 * 1024 * 1024

TM_IN = 512
TM_MIX = 256
TQ = 512
FOX_HEADS_PER_STEP = 4
TM_FFN = 256


def _dot(a, b):
    return jnp.dot(a, b, preferred_element_type=jnp.float32)


def _dot_nt(a, b):
    return lax.dot_general(a, b, (((1,), (1,)), ((), ())), preferred_element_type=jnp.float32)


def _dot_tn(a, b):
    return lax.dot_general(a, b, (((0,), (0,)), ((), ())), preferred_element_type=jnp.float32)


def _split3(x):
    hi = x.astype(jnp.bfloat16)
    r = x - hi.astype(jnp.float32)
    mid = r.astype(jnp.bfloat16)
    lo = (r - mid.astype(jnp.float32)).astype(jnp.bfloat16)
    return hi, mid, lo


def _log_sigmoid(x):
    return jnp.minimum(x, 0.0) - jnp.log(1.0 + jnp.exp(-jnp.abs(x)))


def _const_spec(shape):
    nd = len(shape)
    return pl.BlockSpec(shape, lambda *_: (0,) * nd)


def _inproj_kernel(x_ref, ln_ref, wu_ref, wg_ref, ws_ref, wq_ref, wk_ref, wvt_ref,
                   u_ref, gla_ref, small_ref, q_ref, k_ref, vt_ref):
    x = x_ref[...]
    ms = jnp.mean(x * x, axis=-1, keepdims=True)
    h = (x * lax.rsqrt(ms + EPS) * ln_ref[...]).astype(jnp.bfloat16)
    u_ref[...] = _dot(h, wu_ref[...])
    gla_ref[...] = _dot(h, wg_ref[...]).astype(jnp.bfloat16)
    small_ref[...] = _dot(h, ws_ref[...])
    q_ref[...] = _dot(h, wq_ref[...]).astype(jnp.bfloat16)
    k_ref[...] = _dot(h, wk_ref[...]).astype(jnp.bfloat16)
    vt_ref[...] = _dot_nt(wvt_ref[...], h).astype(jnp.bfloat16)


def _inproj(x2, ln, wu, wg, ws, wq, wk, wvt, batch, seq):
    bt = x2.shape[0]
    tm = TM_IN
    nt = seq // tm
    row = lambda w: pl.BlockSpec((tm, w), lambda i: (i, 0))
    return pl.pallas_call(
        _inproj_kernel,
        grid=(bt // tm,),
        in_specs=[row(D_MODEL), _const_spec((1, D_MODEL)),
                  _const_spec(wu.shape), _const_spec(wg.shape), _const_spec(ws.shape),
                  _const_spec(wq.shape), _const_spec(wk.shape), _const_spec(wvt.shape)],
        out_specs=[row(POOL_WIDTH), row(4 * GLA_WIDTH), row(LANES), row(FOX_WIDTH), row(FOX_WIDTH),
                   pl.BlockSpec((None, FOX_WIDTH, tm), lambda i: (i // nt, 0, i % nt))],
        out_shape=[jax.ShapeDtypeStruct((bt, POOL_WIDTH), jnp.float32),
                   jax.ShapeDtypeStruct((bt, 4 * GLA_WIDTH), jnp.bfloat16),
                   jax.ShapeDtypeStruct((bt, LANES), jnp.float32),
                   jax.ShapeDtypeStruct((bt, FOX_WIDTH), jnp.bfloat16),
                   jax.ShapeDtypeStruct((bt, FOX_WIDTH), jnp.bfloat16),
                   jax.ShapeDtypeStruct((batch, FOX_WIDTH, seq), jnp.bfloat16)],
        compiler_params=pltpu.CompilerParams(dimension_semantics=("arbitrary",),
                                             vmem_limit_bytes=VMEM_LIMIT),
        name="inproj",
    )(x2, ln, wu, wg, ws, wq, wk, wvt)


def _mixpg_kernel(u_ref, gla_ref, small_ref,
                  wpool_ref, pscale_ref, wa_hi_ref, wa_lo_ref, ba_ref, gn_ref, bf_ref,
                  tri_chunk_ref, tri_full_ref, pq_ref, cq_ref, pk_ref, ck_ref,
                  mpg_ref, fq_ref, fk_ref,
                  ubuf, fcarry, st_ref, bcum_ref, og_ref):
    ti = pl.program_id(1)
    tm = TM_MIX

    @pl.when(ti == 0)
    def _():
        ubuf[0:HALO, :] = jnp.zeros((HALO, POOL_WIDTH), jnp.float32)
        fcarry[...] = jnp.zeros_like(fcarry)
        st_ref[...] = jnp.zeros_like(st_ref)

    u = u_ref[...]
    ubuf[HALO:HALO + tm, :] = u
    acc = u
    sums = {}
    for k in range(1, POOL_WINDOWS[-1]):
        acc = acc + ubuf[HALO - k:HALO - k + tm, :]
        if k + 1 in POOL_WINDOWS:
            sums[k + 1] = acc
    lane = lax.broadcasted_iota(jnp.int32, (tm, POOL_WIDTH), 1)
    tpos = ti * tm + lax.broadcasted_iota(jnp.int32, (tm, POOL_WIDTH), 0)
    wsum = sums[POOL_WINDOWS[-1]]
    width = jnp.full((tm, POOL_WIDTH), POOL_WINDOWS[-1], jnp.int32)
    for gi in range(len(POOL_WINDOWS) - 2, -1, -1):
        in_group = lane < (gi + 1) * POOL_GROUP
        wsum = jnp.where(in_group, sums[POOL_WINDOWS[gi]], wsum)
        width = jnp.where(in_group, POOL_WINDOWS[gi], width)
    cnt = jnp.minimum(tpos + 1, width).astype(jnp.float32)
    d = wsum / cnt - u
    y_pool = _dot(d.astype(jnp.bfloat16), wpool_ref[...]) * pscale_ref[...]
    mpg_ref[:, 0:POOL_WIDTH] = y_pool.astype(jnp.bfloat16)
    ubuf[0:HALO, :] = ubuf[tm:tm + HALO, :]

    small = small_ref[...]
    s_hi = small.astype(jnp.bfloat16)
    s_lo = (small - s_hi.astype(jnp.float32)).astype(jnp.bfloat16)
    a = (_dot(s_hi, wa_hi_ref[...]) + _dot(s_lo, wa_hi_ref[...]) + _dot(s_hi, wa_lo_ref[...])
         + ba_ref[...])
    la = _log_sigmoid(a) / GLA_TAU
    la_hi, la_mid, la_lo = _split3(la)
    tri_c = tri_chunk_ref[...]
    bcum_ref[...] = _dot(tri_c, la_hi) + _dot(tri_c, la_mid) + _dot(tri_c, la_lo)

    slane = lax.broadcasted_iota(jnp.int32, (tm, LANES), 1)
    is_ff = (slane >= FF_LANE0) & (slane < FF_LANE0 + FOX_HEADS)
    lf = jnp.where(is_ff, _log_sigmoid(small + bf_ref[...]), 0.0)
    lf_hi, lf_mid, lf_lo = _split3(lf)
    tri_f = tri_full_ref[...]
    fcum = _dot(tri_f, lf_hi) + _dot(tri_f, lf_mid) + _dot(tri_f, lf_lo) + fcarry[...]
    fcarry[...] = fcum[tm - 1:tm, :]
    f_hi, f_mid, f_lo = _split3(fcum * LOG2E)
    fq_ref[...] = (_dot(f_hi, pq_ref[0]) + _dot(f_mid, pq_ref[1]) + _dot(f_lo, pq_ref[2])
                   + cq_ref[...]).astype(jnp.bfloat16)
    fk_ref[...] = (_dot(f_hi, pk_ref[0]) + _dot(f_mid, pk_ref[1]) + _dot(f_lo, pk_ref[2])
                   + ck_ref[...]).astype(jnp.bfloat16)

    glane = lax.broadcasted_iota(jnp.int32, (1, GLA_WIDTH), 1)
    head_masks = [(glane >= h * GLA_HEAD_DIM) & (glane < (h + 1) * GLA_HEAD_DIM)
                  for h in range(GLA_HEADS)]
    r_i = lax.broadcasted_iota(jnp.int32, (GLA_WIDTH, GLA_WIDTH), 0) // GLA_HEAD_DIM
    c_i = lax.broadcasted_iota(jnp.int32, (GLA_WIDTH, GLA_WIDTH), 1) // GLA_HEAD_DIM
    same_head = r_i == c_i
    causal = (lax.broadcasted_iota(jnp.int32, (CHUNK, CHUNK), 0)
              >= lax.broadcasted_iota(jnp.int32, (CHUNK, CHUNK), 1))

    for c in range(tm // CHUNK):
        r0 = c * CHUNK
        q = gla_ref[r0:r0 + CHUNK, 0:GLA_WIDTH].astype(jnp.float32)
        k = gla_ref[r0:r0 + CHUNK, GLA_WIDTH:2 * GLA_WIDTH].astype(jnp.float32)
        v = gla_ref[r0:r0 + CHUNK, 2 * GLA_WIDTH:3 * GLA_WIDTH]
        bc = bcum_ref[r0:r0 + CHUNK, :]
        b_last = bcum_ref[r0 + CHUNK - 1:r0 + CHUNK, :]
        b_mid = bcum_ref[r0 + CHUNK // 2 - 1:r0 + CHUNK // 2, :]
        q_in = (q * jnp.exp(bc - b_mid)).astype(jnp.bfloat16)
        k_in = (k * jnp.exp(b_mid - bc)).astype(jnp.bfloat16)
        k_kv = (k * jnp.exp(b_last - bc)).astype(jnp.bfloat16)
        q_st = (q * jnp.exp(bc)).astype(jnp.bfloat16)
        dec = jnp.exp(b_last)

        st = st_ref[...]
        o = _dot_nt(q_st, st.astype(jnp.bfloat16))
        for h in range(GLA_HEADS):
            att = _dot_nt(jnp.where(head_masks[h], q_in, jnp.zeros_like(q_in)), k_in)
            att = jnp.where(causal, att, 0.0).astype(jnp.bfloat16)
            o = o + jnp.where(head_masks[h], _dot(att, v), 0.0)
        kv_t = _dot_tn(v, k_kv)
        st_ref[...] = st * dec + jnp.where(same_head, kv_t, 0.0)
        og_ref[r0:r0 + CHUNK, :] = o

    o = og_ref[...]
    o2 = o * o
    mean_sq = jnp.zeros_like(o)
    for h in range(GLA_HEADS):
        hs = jnp.sum(jnp.where(head_masks[h], o2, 0.0), axis=-1, keepdims=True) / GLA_HEAD_DIM
        mean_sq = jnp.where(head_masks[h], hs, mean_sq)
    g = gla_ref[:, 3 * GLA_WIDTH:4 * GLA_WIDTH].astype(jnp.float32)
    y_gla = o * lax.rsqrt(mean_sq + EPS) * gn_ref[...] * (g * jax.nn.sigmoid(g))
    mpg_ref[:, POOL_WIDTH:POOL_WIDTH + GLA_WIDTH] = y_gla.astype(jnp.bfloat16)


def _mixpg(u, gla, small, consts, batch, seq):
    tm = TM_MIX
    nt = seq // tm
    row = lambda w: pl.BlockSpec((tm, w), lambda b, i: (b * nt + i, 0))
    bt = u.shape[0]
    return pl.pallas_call(
        _mixpg_kernel,
        grid=(batch, nt),
        in_specs=[row(POOL_WIDTH), row(4 * GLA_WIDTH), row(LANES)] + [_const_spec(c.shape) for c in consts],
        out_specs=[row(POOL_WIDTH + GLA_WIDTH), row(FOX_WIDTH), row(FOX_WIDTH)],
        out_shape=[jax.ShapeDtypeStruct((bt, POOL_WIDTH + GLA_WIDTH), jnp.bfloat16),
                   jax.ShapeDtypeStruct((bt, FOX_WIDTH), jnp.bfloat16),
                   jax.ShapeDtypeStruct((bt, FOX_WIDTH), jnp.bfloat16)],
        scratch_shapes=[pltpu.VMEM((tm + HALO, POOL_WIDTH), jnp.float32),
                        pltpu.VMEM((1, LANES), jnp.float32),
                        pltpu.VMEM((GLA_WIDTH, GLA_WIDTH), jnp.float32),
                        pltpu.VMEM((tm, GLA_WIDTH), jnp.float32),
                        pltpu.VMEM((tm, GLA_WIDTH), jnp.float32)],
        compiler_params=pltpu.CompilerParams(dimension_semantics=("arbitrary", "arbitrary"),
                                             vmem_limit_bytes=VMEM_LIMIT),
        name="mixpg",
    )(u, gla, small, *consts)


def _fox_kernel(q_ref, fq_ref, k_ref, fk_ref, vt_ref, o_ref, *scratch):
    nh = FOX_HEADS_PER_STEP
    qh, s_, p_, bm, m_, al, acc = (scratch[i * nh:(i + 1) * nh] for i in range(7))
    qi = pl.program_id(2)

    lane = lax.broadcasted_iota(jnp.int32, (1, 2 * LANES), 1)
    for h in range(nh):
        pair, sub = divmod(h, 2)
        cols = slice(pair * LANES, (pair + 1) * LANES)
        qext = jnp.concatenate([q_ref[:, cols], fq_ref[:, cols]], axis=1)
        lo = sub * FOX_HEAD_DIM
        flo = LANES + sub * F_SLOT
        own = ((lane >= lo) & (lane < lo + FOX_HEAD_DIM)) | ((lane >= flo) & (lane < flo + F_SLOT))
        qh[h][...] = jnp.where(own, qext, jnp.zeros_like(qext))
        acc[h][...] = jnp.zeros_like(acc[h])
        m_[h][...] = jnp.full(m_[h].shape, NEG_BIG, jnp.float32)
        p_[h][...] = jnp.zeros_like(p_[h])
        al[h][...] = jnp.ones_like(al[h])

    def scores(j, h):
        k0 = pl.multiple_of(j * TQ, TQ)
        cols = slice((h // 2) * LANES, (h // 2 + 1) * LANES)
        kext = jnp.concatenate([k_ref[pl.ds(k0, TQ), cols], fk_ref[pl.ds(k0, TQ), cols]], axis=1)
        s = _dot_nt(kext, qh[h][...])
        s_[h][...] = s
        bm[h][...] = jnp.max(s, axis=0, keepdims=True)

    def softmax(h, diagonal):
        s = s_[h][...]
        if diagonal:
            krow = lax.broadcasted_iota(jnp.int32, (TQ, TQ), 0)
            qcol = lax.broadcasted_iota(jnp.int32, (TQ, TQ), 1)
            s = jnp.where(krow <= qcol, s, NEG_BIG)
            blk_max = jnp.max(s, axis=0, keepdims=True)
        else:
            blk_max = bm[h][...]
        m_old = m_[h][...]
        m_new = jnp.maximum(m_old, blk_max)
        al[h][...] = jnp.exp2(m_old - m_new)
        p_[h][...] = jnp.exp2(s - m_new).astype(jnp.bfloat16)
        m_[h][...] = m_new

    def update(j, h):
        k0 = pl.multiple_of(j * TQ, TQ)
        vth = jnp.concatenate([vt_ref[h * FOX_HEAD_DIM:(h + 1) * FOX_HEAD_DIM, pl.ds(k0, TQ)],
                               jnp.ones((16, TQ), jnp.bfloat16)], axis=0)
        acc[h][...] = al[h][...] * acc[h][...] + _dot(vth, p_[h][...])

    for h in range(nh):
        scores(0, h)

    def body(j, carry):
        prev = jnp.maximum(j - 1, 0)
        for h in range(nh):
            update(prev, h)
            softmax(h, False)
            scores(j + 1, h)
        return carry

    lax.fori_loop(0, qi, body, 0)
    prev = jnp.maximum(qi - 1, 0)
    for h in range(nh):
        update(prev, h)
        softmax(h, True)
    for h in range(nh):
        update(qi, h)
    for pair in range(nh // 2):
        o_t = jnp.concatenate([a[0:FOX_HEAD_DIM, :] / a[FOX_HEAD_DIM:FOX_HEAD_DIM + 1, :]
                               for a in (acc[2 * pair][...], acc[2 * pair + 1][...])], axis=0)
        o_ref[:, pair * LANES:(pair + 1) * LANES] = o_t.T.astype(jnp.bfloat16)


def _fox(q, fq, k, fk, vt, batch, seq):
    nq = seq // TQ
    nh = FOX_HEADS_PER_STEP
    width = nh * FOX_HEAD_DIM
    groups = FOX_HEADS // nh
    qspec = pl.BlockSpec((TQ, width), lambda b, g, i: (b * nq + i, g))
    kspec = pl.BlockSpec((seq, width), lambda b, g, i: (b, g))
    acc_rows = FOX_HEAD_DIM + 16
    return pl.pallas_call(
        _fox_kernel,
        grid=(batch, groups, nq),
        in_specs=[qspec, qspec, kspec, kspec,
                  pl.BlockSpec((None, width, seq), lambda b, g, i: (b, g, 0))],
        out_specs=qspec,
        out_shape=jax.ShapeDtypeStruct(q.shape, jnp.bfloat16),
        scratch_shapes=([pltpu.VMEM((TQ, 2 * LANES), jnp.bfloat16)] * nh
                        + [pltpu.VMEM((TQ, TQ), jnp.float32)] * nh
                        + [pltpu.VMEM((TQ, TQ), jnp.bfloat16)] * nh
                        + [pltpu.VMEM((1, TQ), jnp.float32)] * (3 * nh)
                        + [pltpu.VMEM((acc_rows, TQ), jnp.float32)] * nh),
        compiler_params=pltpu.CompilerParams(
            dimension_semantics=("arbitrary", "arbitrary", "arbitrary"),
            vmem_limit_bytes=VMEM_LIMIT),
        name="fox",
    )(q, fq, k, fk, vt)


def _outffn_kernel(x_ref, mpg_ref, yf_ref, wo_ref, ln2_ref, wgu_ref, wdn_ref, lnf_ref, o_ref, *, final):
    half = POOL_WIDTH + GLA_WIDTH
    x1 = (x_ref[...] + _dot(mpg_ref[...], wo_ref[0:half, :]) + _dot(yf_ref[...], wo_ref[half:, :]))
    ms = jnp.mean(x1 * x1, axis=-1, keepdims=True)
    h = (x1 * lax.rsqrt(ms + EPS) * ln2_ref[...]).astype(jnp.bfloat16)
    gate = _dot(h, wgu_ref[:, 0:D_FF])
    up = _dot(h, wgu_ref[:, D_FF:])
    act = (gate * jax.nn.sigmoid(gate) * up).astype(jnp.bfloat16)
    x2 = x1 + _dot(act, wdn_ref[...])
    if final:
        ms2 = jnp.mean(x2 * x2, axis=-1, keepdims=True)
        x2 = x2 * lax.rsqrt(ms2 + EPS) * lnf_ref[...]
    o_ref[...] = x2


def _outffn(x2, mpg, yf, wo, ln2, wgu, wdn, lnf, final):
    bt = x2.shape[0]
    tm = TM_FFN
    row = lambda w: pl.BlockSpec((tm, w), lambda i: (i, 0))
    single = lambda shape: pl.BlockSpec(shape, lambda i: (0, 0), pipeline_mode=pl.Buffered(1))
    return pl.pallas_call(
        functools.partial(_outffn_kernel, final=final),
        grid=(bt // tm,),
        in_specs=[row(D_MODEL), row(POOL_WIDTH + GLA_WIDTH), row(FOX_WIDTH),
                  single(wo.shape), _const_spec((1, D_MODEL)), single(wgu.shape), single(wdn.shape),
                  _const_spec((1, D_MODEL))],
        out_specs=row(D_MODEL),
        out_shape=jax.ShapeDtypeStruct((bt, D_MODEL), jnp.float32),
        compiler_params=pltpu.CompilerParams(dimension_semantics=("arbitrary",),
                                             vmem_limit_bytes=VMEM_LIMIT),
        name="outffn",
    )(x2, mpg, yf, wo, ln2, wgu, wdn, lnf)


def _placement(dtype=jnp.bfloat16):
    pq = np.zeros((3, LANES, FOX_WIDTH), np.float32)
    pk = np.zeros((3, LANES, FOX_WIDTH), np.float32)
    cq = np.zeros((1, FOX_WIDTH), np.float32)
    ck = np.zeros((1, FOX_WIDTH), np.float32)
    for h in range(FOX_HEADS):
        base = (h // 2) * LANES + (h % 2) * F_SLOT
        for part in range(3):
            pq[part, FF_LANE0 + h, base + part] = 1.0
            pk[part, FF_LANE0 + h, base + 3 + part] = -1.0
            cq[0, base + 3 + part] = 1.0
            ck[0, base + part] = 1.0
    return jnp.asarray(pq, dtype), jnp.asarray(cq), jnp.asarray(pk, dtype), jnp.asarray(ck)


def _triangles(dtype=jnp.bfloat16):
    r = np.arange(TM_MIX)[:, None]
    c = np.arange(TM_MIX)[None, :]
    full = (r >= c)
    chunk = full & (r // CHUNK == c // CHUNK)
    return jnp.asarray(chunk, dtype), jnp.asarray(full, dtype)


@jax.jit
def kernel(x, ln1, w_in, w_pool, pool_scale, w_a_up, b_a, gla_gn, b_f, w_o, ln2, w_gu, w_down, ln_f):
    batch, seq, _ = x.shape
    depth = w_in.shape[0]
    bf16 = jnp.bfloat16
    xf = x.reshape(batch * seq, D_MODEL)

    tri_chunk, tri_full = _triangles()
    pq, cq, pk, ck = _placement()
    group_of = np.arange(POOL_WIDTH) // POOL_GROUP
    same_group = jnp.asarray(group_of[:, None] == group_of[None, :])

    o0 = POOL_WIDTH
    o1 = o0 + 4 * GLA_WIDTH
    o2 = o1 + GLA_GATE_RANK
    o3 = o2 + 3 * FOX_WIDTH
    col_scale = np.ones((o3 + FOX_HEADS,), np.float32)
    col_scale[o0:o0 + GLA_WIDTH] = GLA_HEAD_DIM ** -0.5
    col_scale[o2:o2 + FOX_WIDTH] = FOX_HEAD_DIM ** -0.5 * LOG2E
    for l in range(depth):
        w = (w_in[l] * col_scale).astype(bf16)
        wu = w[:, 0:o0]
        wg = w[:, o0:o1]
        ws = jnp.pad(jnp.concatenate([w[:, o1:o2], w[:, o3:o3 + FOX_HEADS]], axis=1),
                     ((0, 0), (0, LANES - GLA_GATE_RANK - FOX_HEADS)))
        wq = w[:, o2:o2 + FOX_WIDTH]
        wk = w[:, o2 + FOX_WIDTH:o2 + 2 * FOX_WIDTH]
        wvt = w[:, o2 + 2 * FOX_WIDTH:o3].T
        u, gla, small, q, k, vt = _inproj(xf, ln1[l][None, :], wu, wg, ws, wq, wk, wvt, batch, seq)

        wp = w_pool[l].reshape(POOL_WIDTH, POOL_GROUP)
        wpool_bd = jnp.where(same_group, jnp.tile(wp, (1, len(POOL_WINDOWS))), 0.0).astype(bf16)
        wa = jnp.pad(w_a_up[l], ((0, LANES - GLA_GATE_RANK), (0, 0)))
        wa_hi = wa.astype(bf16)
        wa_lo = (wa - wa_hi.astype(jnp.float32)).astype(bf16)
        bf_row = jnp.pad(b_f[l][None, :], ((0, 0), (FF_LANE0, LANES - FF_LANE0 - FOX_HEADS)))
        consts = [wpool_bd, pool_scale[l][None, :], wa_hi, wa_lo, b_a[l][None, :],
                  gla_gn[l][None, :], bf_row, tri_chunk, tri_full, pq, cq, pk, ck]
        mpg, fq, fk = _mixpg(u, gla, small, consts, batch, seq)

        yf = _fox(q, fq, k, fk, vt, batch, seq)

        xf = _outffn(xf, mpg, yf, w_o[l].astype(bf16), ln2[l][None, :], w_gu[l].astype(bf16),
                     w_down[l].astype(bf16), ln_f[None, :], final=(l == depth - 1))
    return xf.reshape(batch, seq, D_MODEL)
```

```python
import functools

import jax
import jax.numpy as jnp
import numpy as np
from jax import lax
from jax.experimental import pallas as pl
from jax.experimental.pallas import tpu as pltpu

D_MODEL = 1024
CHUNK = 64
EPS = 1e-6
POOL_WIDTH = 256
POOL_WINDOWS = (2, 4, 8, 16)
POOL_GROUP = 64
GLA_WIDTH = 256
GLA_HEADS = 4
GLA_HEAD_DIM = 64
GLA_GATE_RANK = 16
GLA_TAU = 16.0
FOX_WIDTH = 512
FOX_HEAD_DIM = 64
FOX_HEADS = 8
D_FF = 2816

LANES = 128
HALO = 16
FF_LANE0 = GLA_GATE_RANK
F_SLOT = 8
NEG_BIG = -1e30
LOG2E = 1.4426950408889634

VMEM_LIMIT = 56 * 1024 * 1024

TM_IN = 512
TM_MIX = 256
TQ = 512
FOX_HEADS_PER_STEP = 4
TM_FFN = 512


def _dot(a, b):
    return jnp.dot(a, b, preferred_element_type=jnp.float32)


def _dot_nt(a, b):
    return lax.dot_general(a, b, (((1,), (1,)), ((), ())), preferred_element_type=jnp.float32)


def _dot_tn(a, b):
    return lax.dot_general(a, b, (((0,), (0,)), ((), ())), preferred_element_type=jnp.float32)


def _split3(x):
    hi = x.astype(jnp.bfloat16)
    r = x - hi.astype(jnp.float32)
    mid = r.astype(jnp.bfloat16)
    lo = (r - mid.astype(jnp.float32)).astype(jnp.bfloat16)
    return hi, mid, lo


def _log_sigmoid(x):
    return jnp.minimum(x, 0.0) - jnp.log(1.0 + jnp.exp(-jnp.abs(x)))


def _const_spec(shape):
    nd = len(shape)
    return pl.BlockSpec(shape, lambda *_: (0,) * nd)


def _inproj_kernel(x_ref, ln_ref, wu_ref, wg_ref, ws_ref, wq_ref, wk_ref, wvt_ref,
                   u_ref, gla_ref, small_ref, q_ref, k_ref, vt_ref):
    x = x_ref[...]
    ms = jnp.mean(x * x, axis=-1, keepdims=True)
    h = (x * lax.rsqrt(ms + EPS) * ln_ref[...]).astype(jnp.bfloat16)
    u_ref[...] = _dot(h, wu_ref[...])
    gla_ref[...] = _dot(h, wg_ref[...]).astype(jnp.bfloat16)
    small_ref[...] = _dot(h, ws_ref[...])
    q_ref[...] = _dot(h, wq_ref[...]).astype(jnp.bfloat16)
    k_ref[...] = _dot(h, wk_ref[...]).astype(jnp.bfloat16)
    vt_ref[...] = _dot_nt(wvt_ref[...], h).astype(jnp.bfloat16)


def _inproj(x2, ln, wu, wg, ws, wq, wk, wvt, batch, seq):
    bt = x2.shape[0]
    tm = TM_IN
    nt = seq // tm
    row = lambda w: pl.BlockSpec((tm, w), lambda i: (i, 0))
    return pl.pallas_call(
        _inproj_kernel,
        grid=(bt // tm,),
        in_specs=[row(D_MODEL), _const_spec((1, D_MODEL)),
                  _const_spec(wu.shape), _const_spec(wg.shape), _const_spec(ws.shape),
                  _const_spec(wq.shape), _const_spec(wk.shape), _const_spec(wvt.shape)],
        out_specs=[row(POOL_WIDTH), row(4 * GLA_WIDTH), row(LANES), row(FOX_WIDTH), row(FOX_WIDTH),
                   pl.BlockSpec((None, FOX_WIDTH, tm), lambda i: (i // nt, 0, i % nt))],
        out_shape=[jax.ShapeDtypeStruct((bt, POOL_WIDTH), jnp.float32),
                   jax.ShapeDtypeStruct((bt, 4 * GLA_WIDTH), jnp.bfloat16),
                   jax.ShapeDtypeStruct((bt, LANES), jnp.float32),
                   jax.ShapeDtypeStruct((bt, FOX_WIDTH), jnp.bfloat16),
                   jax.ShapeDtypeStruct((bt, FOX_WIDTH), jnp.bfloat16),
                   jax.ShapeDtypeStruct((batch, FOX_WIDTH, seq), jnp.bfloat16)],
        compiler_params=pltpu.CompilerParams(dimension_semantics=("arbitrary",),
                                             vmem_limit_bytes=VMEM_LIMIT),
        name="inproj",
    )(x2, ln, wu, wg, ws, wq, wk, wvt)


def _mixpg_kernel(u_ref, gla_ref, small_ref,
                  wpool_ref, pscale_ref, wa_hi_ref, wa_lo_ref, ba_ref, gn_ref, bf_ref,
                  tri_chunk_ref, tri_full_ref, pq_ref, cq_ref, pk_ref, ck_ref,
                  mpg_ref, fq_ref, fk_ref,
                  ubuf, fcarry, st_ref, bcum_ref, og_ref):
    ti = pl.program_id(1)
    tm = TM_MIX

    @pl.when(ti == 0)
    def _():
        ubuf[0:HALO, :] = jnp.zeros((HALO, POOL_WIDTH), jnp.float32)
        fcarry[...] = jnp.zeros_like(fcarry)
        st_ref[...] = jnp.zeros_like(st_ref)

    u = u_ref[...]
    ubuf[HALO:HALO + tm, :] = u
    acc = u
    sums = {}
    for k in range(1, POOL_WINDOWS[-1]):
        acc = acc + ubuf[HALO - k:HALO - k + tm, :]
        if k + 1 in POOL_WINDOWS:
            sums[k + 1] = acc
    lane = lax.broadcasted_iota(jnp.int32, (tm, POOL_WIDTH), 1)
    tpos = ti * tm + lax.broadcasted_iota(jnp.int32, (tm, POOL_WIDTH), 0)
    wsum = sums[POOL_WINDOWS[-1]]
    width = jnp.full((tm, POOL_WIDTH), POOL_WINDOWS[-1], jnp.int32)
    for gi in range(len(POOL_WINDOWS) - 2, -1, -1):
        in_group = lane < (gi + 1) * POOL_GROUP
        wsum = jnp.where(in_group, sums[POOL_WINDOWS[gi]], wsum)
        width = jnp.where(in_group, POOL_WINDOWS[gi], width)
    cnt = jnp.minimum(tpos + 1, width).astype(jnp.float32)
    d = wsum / cnt - u
    y_pool = _dot(d.astype(jnp.bfloat16), wpool_ref[...]) * pscale_ref[...]
    mpg_ref[:, 0:POOL_WIDTH] = y_pool.astype(jnp.bfloat16)
    ubuf[0:HALO, :] = ubuf[tm:tm + HALO, :]

    small = small_ref[...]
    s_hi = small.astype(jnp.bfloat16)
    s_lo = (small - s_hi.astype(jnp.float32)).astype(jnp.bfloat16)
    a = (_dot(s_hi, wa_hi_ref[...]) + _dot(s_lo, wa_hi_ref[...]) + _dot(s_hi, wa_lo_ref[...])
         + ba_ref[...])
    la = _log_sigmoid(a) / GLA_TAU
    la_hi, la_mid, la_lo = _split3(la)
    tri_c = tri_chunk_ref[...]
    bcum_ref[...] = _dot(tri_c, la_hi) + _dot(tri_c, la_mid) + _dot(tri_c, la_lo)

    slane = lax.broadcasted_iota(jnp.int32, (tm, LANES), 1)
    is_ff = (slane >= FF_LANE0) & (slane < FF_LANE0 + FOX_HEADS)
    lf = jnp.where(is_ff, _log_sigmoid(small + bf_ref[...]), 0.0)
    lf_hi, lf_mid, lf_lo = _split3(lf)
    tri_f = tri_full_ref[...]
    fcum = _dot(tri_f, lf_hi) + _dot(tri_f, lf_mid) + _dot(tri_f, lf_lo) + fcarry[...]
    fcarry[...] = fcum[tm - 1:tm, :]
    f_hi, f_mid, f_lo = _split3(fcum * LOG2E)
    fq_ref[...] = (_dot(f_hi, pq_ref[0]) + _dot(f_mid, pq_ref[1]) + _dot(f_lo, pq_ref[2])
                   + cq_ref[...]).astype(jnp.bfloat16)
    fk_ref[...] = (_dot(f_hi, pk_ref[0]) + _dot(f_mid, pk_ref[1]) + _dot(f_lo, pk_ref[2])
                   + ck_ref[...]).astype(jnp.bfloat16)

    glane = lax.broadcasted_iota(jnp.int32, (1, GLA_WIDTH), 1)
    head_masks = [(glane >= h * GLA_HEAD_DIM) & (glane < (h + 1) * GLA_HEAD_DIM)
                  for h in range(GLA_HEADS)]
    r_i = lax.broadcasted_iota(jnp.int32, (GLA_WIDTH, GLA_WIDTH), 0) // GLA_HEAD_DIM
    c_i = lax.broadcasted_iota(jnp.int32, (GLA_WIDTH, GLA_WIDTH), 1) // GLA_HEAD_DIM
    same_head = r_i == c_i
    causal = (lax.broadcasted_iota(jnp.int32, (CHUNK, CHUNK), 0)
              >= lax.broadcasted_iota(jnp.int32, (CHUNK, CHUNK), 1))

    for c in range(tm // CHUNK):
        r0 = c * CHUNK
        q = gla_ref[r0:r0 + CHUNK, 0:GLA_WIDTH].astype(jnp.float32)
        k = gla_ref[r0:r0 + CHUNK, GLA_WIDTH:2 * GLA_WIDTH].astype(jnp.float32)
        v = gla_ref[r0:r0 + CHUNK, 2 * GLA_WIDTH:3 * GLA_WIDTH]
        bc = bcum_ref[r0:r0 + CHUNK, :]
        b_last = bcum_ref[r0 + CHUNK - 1:r0 + CHUNK, :]
        b_mid = bcum_ref[r0 + CHUNK // 2 - 1:r0 + CHUNK // 2, :]
        q_in = (q * jnp.exp(bc - b_mid)).astype(jnp.bfloat16)
        k_in = (k * jnp.exp(b_mid - bc)).astype(jnp.bfloat16)
        k_kv = (k * jnp.exp(b_last - bc)).astype(jnp.bfloat16)
        q_st = (q * jnp.exp(bc)).astype(jnp.bfloat16)
        dec = jnp.exp(b_last)

        st = st_ref[...]
        o = _dot_nt(q_st, st.astype(jnp.bfloat16))
        for h in range(GLA_HEADS):
            att = _dot_nt(jnp.where(head_masks[h], q_in, jnp.zeros_like(q_in)), k_in)
            att = jnp.where(causal, att, 0.0).astype(jnp.bfloat16)
            o = o + jnp.where(head_masks[h], _dot(att, v), 0.0)
        kv_t = _dot_tn(v, k_kv)
        st_ref[...] = st * dec + jnp.where(same_head, kv_t, 0.0)
        og_ref[r0:r0 + CHUNK, :] = o

    o = og_ref[...]
    o2 = o * o
    mean_sq = jnp.zeros_like(o)
    for h in range(GLA_HEADS):
        hs = jnp.sum(jnp.where(head_masks[h], o2, 0.0), axis=-1, keepdims=True) / GLA_HEAD_DIM
        mean_sq = jnp.where(head_masks[h], hs, mean_sq)
    g = gla_ref[:, 3 * GLA_WIDTH:4 * GLA_WIDTH].astype(jnp.float32)
    y_gla = o * lax.rsqrt(mean_sq + EPS) * gn_ref[...] * (g * jax.nn.sigmoid(g))
    mpg_ref[:, POOL_WIDTH:POOL_WIDTH + GLA_WIDTH] = y_gla.astype(jnp.bfloat16)


def _mixpg(u, gla, small, consts, batch, seq):
    tm = TM_MIX
    nt = seq // tm
    row = lambda w: pl.BlockSpec((tm, w), lambda b, i: (b * nt + i, 0))
    bt = u.shape[0]
    return pl.pallas_call(
        _mixpg_kernel,
        grid=(batch, nt),
        in_specs=[row(POOL_WIDTH), row(4 * GLA_WIDTH), row(LANES)] + [_const_spec(c.shape) for c in consts],
        out_specs=[row(POOL_WIDTH + GLA_WIDTH), row(FOX_WIDTH), row(FOX_WIDTH)],
        out_shape=[jax.ShapeDtypeStruct((bt, POOL_WIDTH + GLA_WIDTH), jnp.bfloat16),
                   jax.ShapeDtypeStruct((bt, FOX_WIDTH), jnp.bfloat16),
                   jax.ShapeDtypeStruct((bt, FOX_WIDTH), jnp.bfloat16)],
        scratch_shapes=[pltpu.VMEM((tm + HALO, POOL_WIDTH), jnp.float32),
                        pltpu.VMEM((1, LANES), jnp.float32),
                        pltpu.VMEM((GLA_WIDTH, GLA_WIDTH), jnp.float32),
                        pltpu.VMEM((tm, GLA_WIDTH), jnp.float32),
                        pltpu.VMEM((tm, GLA_WIDTH), jnp.float32)],
        compiler_params=pltpu.CompilerParams(dimension_semantics=("arbitrary", "arbitrary"),
                                             vmem_limit_bytes=VMEM_LIMIT),
        name="mixpg",
    )(u, gla, small, *consts)


def _fox_kernel(q_ref, fq_ref, k_ref, fk_ref, vt_ref, o_ref, *scratch):
    nh = FOX_HEADS_PER_STEP
    qh, s_, p_, bm, m_, al, acc = (scratch[i * nh:(i + 1) * nh] for i in range(7))
    qi = pl.program_id(2)

    row = lax.broadcasted_iota(jnp.int32, (2 * LANES, 1), 0)
    for h in range(nh):
        pair, sub = divmod(h, 2)
        cols = slice(pair * LANES, (pair + 1) * LANES)
        qext_t = jnp.concatenate([q_ref[:, cols], fq_ref[:, cols]], axis=1).astype(jnp.float32).T
        lo = sub * FOX_HEAD_DIM
        flo = LANES + sub * F_SLOT
        own = ((row >= lo) & (row < lo + FOX_HEAD_DIM)) | ((row >= flo) & (row < flo + F_SLOT))
        qh[h][...] = jnp.where(own, qext_t, 0.0).astype(jnp.bfloat16)
        acc[h][...] = jnp.zeros_like(acc[h])
        m_[h][...] = jnp.full(m_[h].shape, NEG_BIG, jnp.float32)
        p_[h][...] = jnp.zeros_like(p_[h])
        al[h][...] = jnp.ones_like(al[h])

    def scores(j, h):
        k0 = pl.multiple_of(j * TQ, TQ)
        cols = slice((h // 2) * LANES, (h // 2 + 1) * LANES)
        kext = jnp.concatenate([k_ref[pl.ds(k0, TQ), cols], fk_ref[pl.ds(k0, TQ), cols]], axis=1)
        s = _dot(kext, qh[h][...])
        s_[h][...] = s
        bm[h][...] = jnp.max(s, axis=0, keepdims=True)

    def softmax(h, diagonal):
        s = s_[h][...]
        if diagonal:
            krow = lax.broadcasted_iota(jnp.int32, (TQ, TQ), 0)
            qcol = lax.broadcasted_iota(jnp.int32, (TQ, TQ), 1)
            s = jnp.where(krow <= qcol, s, NEG_BIG)
            blk_max = jnp.max(s, axis=0, keepdims=True)
        else:
            blk_max = bm[h][...]
        m_old = m_[h][...]
        m_new = jnp.maximum(m_old, blk_max)
        al[h][...] = jnp.exp2(m_old - m_new)
        p_[h][...] = jnp.exp2((s - m_new).astype(jnp.bfloat16))
        m_[h][...] = m_new

    def update(j, h):
        k0 = pl.multiple_of(j * TQ, TQ)
        vth = jnp.concatenate([vt_ref[h * FOX_HEAD_DIM:(h + 1) * FOX_HEAD_DIM, pl.ds(k0, TQ)],
                               jnp.ones((16, TQ), jnp.bfloat16)], axis=0)
        acc[h][...] = al[h][...] * acc[h][...] + _dot(vth, p_[h][...])

    for h in range(nh):
        scores(0, h)

    def body(j, carry):
        prev = jnp.maximum(j - 1, 0)
        for h in range(nh):
            update(prev, h)
            softmax(h, False)
            scores(j + 1, h)
        return carry

    lax.fori_loop(0, qi, body, 0)
    prev = jnp.maximum(qi - 1, 0)
    for h in range(nh):
        update(prev, h)
        softmax(h, True)
    for h in range(nh):
        update(qi, h)
    for pair in range(nh // 2):
        o_t = jnp.concatenate([a[0:FOX_HEAD_DIM, :] / a[FOX_HEAD_DIM:FOX_HEAD_DIM + 1, :]
                               for a in (acc[2 * pair][...], acc[2 * pair + 1][...])], axis=0)
        o_ref[:, pair * LANES:(pair + 1) * LANES] = o_t.T.astype(jnp.bfloat16)


def _fox(q, fq, k, fk, vt, batch, seq):
    nq = seq // TQ
    nh = FOX_HEADS_PER_STEP
    width = nh * FOX_HEAD_DIM
    groups = FOX_HEADS // nh
    qspec = pl.BlockSpec((TQ, width), lambda b, g, i: (b * nq + i, g))
    kspec = pl.BlockSpec((seq, width), lambda b, g, i: (b, g))
    acc_rows = FOX_HEAD_DIM + 16
    return pl.pallas_call(
        _fox_kernel,
        grid=(batch, groups, nq),
        in_specs=[qspec, qspec, kspec, kspec,
                  pl.BlockSpec((None, width, seq), lambda b, g, i: (b, g, 0))],
        out_specs=qspec,
        out_shape=jax.ShapeDtypeStruct(q.shape, jnp.bfloat16),
        scratch_shapes=([pltpu.VMEM((2 * LANES, TQ), jnp.bfloat16)] * nh
                        + [pltpu.VMEM((TQ, TQ), jnp.float32)] * nh
                        + [pltpu.VMEM((TQ, TQ), jnp.bfloat16)] * nh
                        + [pltpu.VMEM((1, TQ), jnp.float32)] * (3 * nh)
                        + [pltpu.VMEM((acc_rows, TQ), jnp.float32)] * nh),
        compiler_params=pltpu.CompilerParams(
            dimension_semantics=("arbitrary", "arbitrary", "arbitrary"),
            vmem_limit_bytes=VMEM_LIMIT),
        name="fox",
    )(q, fq, k, fk, vt)


def _outffn_kernel(x_ref, mpg_ref, yf_ref, wo_ref, ln2_ref, wgu_ref, wdn_ref, lnf_ref, o_ref, *, final):
    half = POOL_WIDTH + GLA_WIDTH
    x1 = (x_ref[...] + _dot(mpg_ref[...], wo_ref[0:half, :]) + _dot(yf_ref[...], wo_ref[half:, :]))
    ms = jnp.mean(x1 * x1, axis=-1, keepdims=True)
    h = (x1 * lax.rsqrt(ms + EPS) * ln2_ref[...]).astype(jnp.bfloat16)
    gate = _dot(h, wgu_ref[:, 0:D_FF])
    up = _dot(h, wgu_ref[:, D_FF:])
    act = (gate * jax.nn.sigmoid(gate) * up).astype(jnp.bfloat16)
    x2 = x1 + _dot(act, wdn_ref[...])
    if final:
        ms2 = jnp.mean(x2 * x2, axis=-1, keepdims=True)
        x2 = x2 * lax.rsqrt(ms2 + EPS) * lnf_ref[...]
    o_ref[...] = x2


def _outffn(x2, mpg, yf, wo, ln2, wgu, wdn, lnf, final):
    bt = x2.shape[0]
    tm = TM_FFN
    row = lambda w: pl.BlockSpec((tm, w), lambda i: (i, 0))
    single = lambda shape: pl.BlockSpec(shape, lambda i: (0, 0), pipeline_mode=pl.Buffered(1))
    return pl.pallas_call(
        functools.partial(_outffn_kernel, final=final),
        grid=(bt // tm,),
        in_specs=[row(D_MODEL), row(POOL_WIDTH + GLA_WIDTH), row(FOX_WIDTH),
                  single(wo.shape), _const_spec((1, D_MODEL)), single(wgu.shape), single(wdn.shape),
                  _const_spec((1, D_MODEL))],
        out_specs=row(D_MODEL),
        out_shape=jax.ShapeDtypeStruct((bt, D_MODEL), jnp.float32),
        compiler_params=pltpu.CompilerParams(dimension_semantics=("arbitrary",),
                                             vmem_limit_bytes=VMEM_LIMIT),
        name="outffn",
    )(x2, mpg, yf, wo, ln2, wgu, wdn, lnf)


def _placement(dtype=jnp.bfloat16):
    pq = np.zeros((3, LANES, FOX_WIDTH), np.float32)
    pk = np.zeros((3, LANES, FOX_WIDTH), np.float32)
    cq = np.zeros((1, FOX_WIDTH), np.float32)
    ck = np.zeros((1, FOX_WIDTH), np.float32)
    for h in range(FOX_HEADS):
        base = (h // 2) * LANES + (h % 2) * F_SLOT
        for part in range(3):
            pq[part, FF_LANE0 + h, base + part] = 1.0
            pk[part, FF_LANE0 + h, base + 3 + part] = -1.0
            cq[0, base + 3 + part] = 1.0
            ck[0, base + part] = 1.0
    return jnp.asarray(pq, dtype), jnp.asarray(cq), jnp.asarray(pk, dtype), jnp.asarray(ck)


def _triangles(dtype=jnp.bfloat16):
    r = np.arange(TM_MIX)[:, None]
    c = np.arange(TM_MIX)[None, :]
    full = (r >= c)
    chunk = full & (r // CHUNK == c // CHUNK)
    return jnp.asarray(chunk, dtype), jnp.asarray(full, dtype)


@jax.jit
def kernel(x, ln1, w_in, w_pool, pool_scale, w_a_up, b_a, gla_gn, b_f, w_o, ln2, w_gu, w_down, ln_f):
    batch, seq, _ = x.shape
    depth = w_in.shape[0]
    bf16 = jnp.bfloat16
    xf = x.reshape(batch * seq, D_MODEL)

    tri_chunk, tri_full = _triangles()
    pq, cq, pk, ck = _placement()
    group_of = np.arange(POOL_WIDTH) // POOL_GROUP
    same_group = jnp.asarray(group_of[:, None] == group_of[None, :])

    o0 = POOL_WIDTH
    o1 = o0 + 4 * GLA_WIDTH
    o2 = o1 + GLA_GATE_RANK
    o3 = o2 + 3 * FOX_WIDTH
    col_scale = np.ones((o3 + FOX_HEADS,), np.float32)
    col_scale[o0:o0 + GLA_WIDTH] = GLA_HEAD_DIM ** -0.5
    col_scale[o2:o2 + FOX_WIDTH] = FOX_HEAD_DIM ** -0.5 * LOG2E
    for l in range(depth):
        w = (w_in[l] * col_scale).astype(bf16)
        wu = w[:, 0:o0]
        wg = w[:, o0:o1]
        ws = jnp.pad(jnp.concatenate([w[:, o1:o2], w[:, o3:o3 + FOX_HEADS]], axis=1),
                     ((0, 0), (0, LANES - GLA_GATE_RANK - FOX_HEADS)))
        wq = w[:, o2:o2 + FOX_WIDTH]
        wk = w[:, o2 + FOX_WIDTH:o2 + 2 * FOX_WIDTH]
        wvt = w[:, o2 + 2 * FOX_WIDTH:o3].T
        u, gla, small, q, k, vt = _inproj(xf, ln1[l][None, :], wu, wg, ws, wq, wk, wvt, batch, seq)

        wp = w_pool[l].reshape(POOL_WIDTH, POOL_GROUP)
        wpool_bd = jnp.where(same_group, jnp.tile(wp, (1, len(POOL_WINDOWS))), 0.0).astype(bf16)
        wa = jnp.pad(w_a_up[l], ((0, LANES - GLA_GATE_RANK), (0, 0)))
        wa_hi = wa.astype(bf16)
        wa_lo = (wa - wa_hi.astype(jnp.float32)).astype(bf16)
        bf_row = jnp.pad(b_f[l][None, :], ((0, 0), (FF_LANE0, LANES - FF_LANE0 - FOX_HEADS)))
        consts = [wpool_bd, pool_scale[l][None, :], wa_hi, wa_lo, b_a[l][None, :],
                  gla_gn[l][None, :], bf_row, tri_chunk, tri_full, pq, cq, pk, ck]
        mpg, fq, fk = _mixpg(u, gla, small, consts, batch, seq)

        yf = _fox(q, fq, k, fk, vt, batch, seq)

        xf = _outffn(xf, mpg, yf, w_o[l].astype(bf16), ln2[l][None, :], w_gu[l].astype(bf16),
                     w_down[l].astype(bf16), ln_f[None, :], final=(l == depth - 1))
    return xf.reshape(batch, seq, D_MODEL)
```

```python
import functools

import jax
import jax.numpy as jnp
import numpy as np
from jax import lax
from jax.experimental import pallas as pl
from jax.experimental.pallas import tpu as pltpu

D_MODEL = 1024
CHUNK = 64
EPS = 1e-6
POOL_WIDTH = 256
POOL_WINDOWS = (2, 4, 8, 16)
POOL_GROUP = 64
GLA_WIDTH = 256
GLA_HEADS = 4
GLA_HEAD_DIM = 64
GLA_GATE_RANK = 16
GLA_TAU = 16.0
FOX_WIDTH = 512
FOX_HEAD_DIM = 64
FOX_HEADS = 8
D_FF = 2816

LANES = 128
HALO = 16
FF_LANE0 = GLA_GATE_RANK
F_SLOT = 8
NEG_BIG = -1e30
LOG2E = 1.4426950408889634
PRUNE_LOG2 = 150.0
NORM_SLACK = 1.01

VMEM_LIMIT = 56 * 1024 * 1024

TM_IN = 512
TM_MIX = 256
TQ = 512
FOX_HEADS_PER_STEP = 4
TM_FFN = 512


def _dot(a, b):
    return jnp.dot(a, b, preferred_element_type=jnp.float32)


def _dot_nt(a, b):
    return lax.dot_general(a, b, (((1,), (1,)), ((), ())), preferred_element_type=jnp.float32)


def _dot_tn(a, b):
    return lax.dot_general(a, b, (((0,), (0,)), ((), ())), preferred_element_type=jnp.float32)


def _split3(x):
    hi = x.astype(jnp.bfloat16)
    r = x - hi.astype(jnp.float32)
    mid = r.astype(jnp.bfloat16)
    lo = (r - mid.astype(jnp.float32)).astype(jnp.bfloat16)
    return hi, mid, lo


def _log_sigmoid(x):
    return jnp.minimum(x, 0.0) - jnp.log(1.0 + jnp.exp(-jnp.abs(x)))


def _const_spec(shape):
    nd = len(shape)
    return pl.BlockSpec(shape, lambda *_: (0,) * nd)


def _inproj_kernel(x_ref, ln_ref, wu_ref, wg_ref, ws_ref, wq_ref, wk_ref, wvt_ref, hsum_ref,
                   u_ref, gla_ref, small_ref, q_ref, k_ref, vt_ref, nrm_ref):
    x = x_ref[...]
    ms = jnp.mean(x * x, axis=-1, keepdims=True)
    h = (x * lax.rsqrt(ms + EPS) * ln_ref[...]).astype(jnp.bfloat16)
    u_ref[...] = _dot(h, wu_ref[...])
    gla_ref[...] = _dot(h, wg_ref[...]).astype(jnp.bfloat16)
    small_ref[...] = _dot(h, ws_ref[...])
    qb = _dot(h, wq_ref[...]).astype(jnp.bfloat16)
    kb = _dot(h, wk_ref[...]).astype(jnp.bfloat16)
    q_ref[...] = qb
    k_ref[...] = kb
    vt_ref[...] = _dot_nt(wvt_ref[...], h).astype(jnp.bfloat16)

    def max_sq_norm(zb):
        z = zb.astype(jnp.float32)
        n2 = _dot((z * z).astype(jnp.bfloat16), hsum_ref[...])
        return jnp.max(n2, axis=0, keepdims=True) * NORM_SLACK

    nrm_ref[...] = jnp.concatenate([max_sq_norm(qb), max_sq_norm(kb)], axis=0)


def _inproj(x2, ln, wu, wg, ws, wq, wk, wvt, hsum, batch, seq):
    bt = x2.shape[0]
    tm = TM_IN
    nt = seq // tm
    row = lambda w: pl.BlockSpec((tm, w), lambda i: (i, 0))
    return pl.pallas_call(
        _inproj_kernel,
        grid=(bt // tm,),
        in_specs=[row(D_MODEL), _const_spec((1, D_MODEL)),
                  _const_spec(wu.shape), _const_spec(wg.shape), _const_spec(ws.shape),
                  _const_spec(wq.shape), _const_spec(wk.shape), _const_spec(wvt.shape),
                  _const_spec(hsum.shape)],
        out_specs=[row(POOL_WIDTH), row(4 * GLA_WIDTH), row(LANES), row(FOX_WIDTH), row(FOX_WIDTH),
                   pl.BlockSpec((None, FOX_WIDTH, tm), lambda i: (i // nt, 0, i % nt)),
                   pl.BlockSpec((None, 2, LANES), lambda i: (i, 0, 0))],
        out_shape=[jax.ShapeDtypeStruct((bt, POOL_WIDTH), jnp.float32),
                   jax.ShapeDtypeStruct((bt, 4 * GLA_WIDTH), jnp.bfloat16),
                   jax.ShapeDtypeStruct((bt, LANES), jnp.float32),
                   jax.ShapeDtypeStruct((bt, FOX_WIDTH), jnp.bfloat16),
                   jax.ShapeDtypeStruct((bt, FOX_WIDTH), jnp.bfloat16),
                   jax.ShapeDtypeStruct((batch, FOX_WIDTH, seq), jnp.bfloat16),
                   jax.ShapeDtypeStruct((bt // tm, 2, LANES), jnp.float32)],
        compiler_params=pltpu.CompilerParams(dimension_semantics=("arbitrary",),
                                             vmem_limit_bytes=VMEM_LIMIT),
        name="inproj",
    )(x2, ln, wu, wg, ws, wq, wk, wvt, hsum)


def _mixpg_kernel(u_ref, gla_ref, small_ref,
                  wpool_ref, pscale_ref, wa_hi_ref, wa_lo_ref, ba_ref, gn_ref, bf_ref,
                  tri_chunk_ref, tri_full_ref, pq_ref, cq_ref, pk_ref, ck_ref,
                  mpg_ref, fq_ref, fk_ref, fedge_ref,
                  ubuf, fcarry, st_ref, bcum_ref, og_ref):
    ti = pl.program_id(1)
    tm = TM_MIX

    @pl.when(ti == 0)
    def _():
        ubuf[0:HALO, :] = jnp.zeros((HALO, POOL_WIDTH), jnp.float32)
        fcarry[...] = jnp.zeros_like(fcarry)
        st_ref[...] = jnp.zeros_like(st_ref)

    u = u_ref[...]
    ubuf[HALO:HALO + tm, :] = u
    acc = u
    sums = {}
    for k in range(1, POOL_WINDOWS[-1]):
        acc = acc + ubuf[HALO - k:HALO - k + tm, :]
        if k + 1 in POOL_WINDOWS:
            sums[k + 1] = acc
    lane = lax.broadcasted_iota(jnp.int32, (tm, POOL_WIDTH), 1)
    tpos = ti * tm + lax.broadcasted_iota(jnp.int32, (tm, POOL_WIDTH), 0)
    wsum = sums[POOL_WINDOWS[-1]]
    width = jnp.full((tm, POOL_WIDTH), POOL_WINDOWS[-1], jnp.int32)
    for gi in range(len(POOL_WINDOWS) - 2, -1, -1):
        in_group = lane < (gi + 1) * POOL_GROUP
        wsum = jnp.where(in_group, sums[POOL_WINDOWS[gi]], wsum)
        width = jnp.where(in_group, POOL_WINDOWS[gi], width)
    cnt = jnp.minimum(tpos + 1, width).astype(jnp.float32)
    d = wsum / cnt - u
    y_pool = _dot(d.astype(jnp.bfloat16), wpool_ref[...]) * pscale_ref[...]
    mpg_ref[:, 0:POOL_WIDTH] = y_pool.astype(jnp.bfloat16)
    ubuf[0:HALO, :] = ubuf[tm:tm + HALO, :]

    small = small_ref[...]
    s_hi = small.astype(jnp.bfloat16)
    s_lo = (small - s_hi.astype(jnp.float32)).astype(jnp.bfloat16)
    a = (_dot(s_hi, wa_hi_ref[...]) + _dot(s_lo, wa_hi_ref[...]) + _dot(s_hi, wa_lo_ref[...])
         + ba_ref[...])
    la = _log_sigmoid(a) / GLA_TAU
    la_hi, la_mid, la_lo = _split3(la)
    tri_c = tri_chunk_ref[...]
    bcum_ref[...] = _dot(tri_c, la_hi) + _dot(tri_c, la_mid) + _dot(tri_c, la_lo)

    slane = lax.broadcasted_iota(jnp.int32, (tm, LANES), 1)
    is_ff = (slane >= FF_LANE0) & (slane < FF_LANE0 + FOX_HEADS)
    lf = jnp.where(is_ff, _log_sigmoid(small + bf_ref[...]), 0.0)
    lf_hi, lf_mid, lf_lo = _split3(lf)
    tri_f = tri_full_ref[...]
    fcum = _dot(tri_f, lf_hi) + _dot(tri_f, lf_mid) + _dot(tri_f, lf_lo) + fcarry[...]
    fcarry[...] = fcum[tm - 1:tm, :]
    fl = fcum * LOG2E
    fedge_ref[...] = jnp.concatenate([fl[0:1, :], fl[tm - 1:tm, :]], axis=0)
    f_hi, f_mid, f_lo = _split3(fl)
    fq_ref[...] = (_dot(f_hi, pq_ref[0]) + _dot(f_mid, pq_ref[1]) + _dot(f_lo, pq_ref[2])
                   + cq_ref[...]).astype(jnp.bfloat16)
    fk_ref[...] = (_dot(f_hi, pk_ref[0]) + _dot(f_mid, pk_ref[1]) + _dot(f_lo, pk_ref[2])
                   + ck_ref[...]).astype(jnp.bfloat16)

    glane = lax.broadcasted_iota(jnp.int32, (1, GLA_WIDTH), 1)
    head_masks = [(glane >= h * GLA_HEAD_DIM) & (glane < (h + 1) * GLA_HEAD_DIM)
                  for h in range(GLA_HEADS)]
    r_i = lax.broadcasted_iota(jnp.int32, (GLA_WIDTH, GLA_WIDTH), 0) // GLA_HEAD_DIM
    c_i = lax.broadcasted_iota(jnp.int32, (GLA_WIDTH, GLA_WIDTH), 1) // GLA_HEAD_DIM
    same_head = r_i == c_i
    causal = (lax.broadcasted_iota(jnp.int32, (CHUNK, CHUNK), 0)
              >= lax.broadcasted_iota(jnp.int32, (CHUNK, CHUNK), 1))

    for c in range(tm // CHUNK):
        r0 = c * CHUNK
        q = gla_ref[r0:r0 + CHUNK, 0:GLA_WIDTH].astype(jnp.float32)
        k = gla_ref[r0:r0 + CHUNK, GLA_WIDTH:2 * GLA_WIDTH].astype(jnp.float32)
        v = gla_ref[r0:r0 + CHUNK, 2 * GLA_WIDTH:3 * GLA_WIDTH]
        bc = bcum_ref[r0:r0 + CHUNK, :]
        b_last = bcum_ref[r0 + CHUNK - 1:r0 + CHUNK, :]
        b_mid = bcum_ref[r0 + CHUNK // 2 - 1:r0 + CHUNK // 2, :]
        q_in = (q * jnp.exp(bc - b_mid)).astype(jnp.bfloat16)
        k_in = (k * jnp.exp(b_mid - bc)).astype(jnp.bfloat16)
        k_kv = (k * jnp.exp(b_last - bc)).astype(jnp.bfloat16)
        q_st = (q * jnp.exp(bc)).astype(jnp.bfloat16)
        dec = jnp.exp(b_last)

        st = st_ref[...]
        o = _dot_nt(q_st, st.astype(jnp.bfloat16))
        for h in range(GLA_HEADS):
            att = _dot_nt(jnp.where(head_masks[h], q_in, jnp.zeros_like(q_in)), k_in)
            att = jnp.where(causal, att, 0.0).astype(jnp.bfloat16)
            o = o + jnp.where(head_masks[h], _dot(att, v), 0.0)
        kv_t = _dot_tn(v, k_kv)
        st_ref[...] = st * dec + jnp.where(same_head, kv_t, 0.0)
        og_ref[r0:r0 + CHUNK, :] = o

    o = og_ref[...]
    o2 = o * o
    mean_sq = jnp.zeros_like(o)
    for h in range(GLA_HEADS):
        hs = jnp.sum(jnp.where(head_masks[h], o2, 0.0), axis=-1, keepdims=True) / GLA_HEAD_DIM
        mean_sq = jnp.where(head_masks[h], hs, mean_sq)
    g = gla_ref[:, 3 * GLA_WIDTH:4 * GLA_WIDTH].astype(jnp.float32)
    y_gla = o * lax.rsqrt(mean_sq + EPS) * gn_ref[...] * (g * jax.nn.sigmoid(g))
    mpg_ref[:, POOL_WIDTH:POOL_WIDTH + GLA_WIDTH] = y_gla.astype(jnp.bfloat16)


def _mixpg(u, gla, small, consts, batch, seq):
    tm = TM_MIX
    nt = seq // tm
    row = lambda w: pl.BlockSpec((tm, w), lambda b, i: (b * nt + i, 0))
    bt = u.shape[0]
    return pl.pallas_call(
        _mixpg_kernel,
        grid=(batch, nt),
        in_specs=[row(POOL_WIDTH), row(4 * GLA_WIDTH), row(LANES)] + [_const_spec(c.shape) for c in consts],
        out_specs=[row(POOL_WIDTH + GLA_WIDTH), row(FOX_WIDTH), row(FOX_WIDTH),
                   pl.BlockSpec((None, 2, LANES), lambda b, i: (b * nt + i, 0, 0))],
        out_shape=[jax.ShapeDtypeStruct((bt, POOL_WIDTH + GLA_WIDTH), jnp.bfloat16),
                   jax.ShapeDtypeStruct((bt, FOX_WIDTH), jnp.bfloat16),
                   jax.ShapeDtypeStruct((bt, FOX_WIDTH), jnp.bfloat16),
                   jax.ShapeDtypeStruct((bt // tm, 2, LANES), jnp.float32)],
        scratch_shapes=[pltpu.VMEM((tm + HALO, POOL_WIDTH), jnp.float32),
                        pltpu.VMEM((1, LANES), jnp.float32),
                        pltpu.VMEM((GLA_WIDTH, GLA_WIDTH), jnp.float32),
                        pltpu.VMEM((tm, GLA_WIDTH), jnp.float32),
                        pltpu.VMEM((tm, GLA_WIDTH), jnp.float32)],
        compiler_params=pltpu.CompilerParams(dimension_semantics=("arbitrary", "arbitrary"),
                                             vmem_limit_bytes=VMEM_LIMIT),
        name="mixpg",
    )(u, gla, small, *consts)


def _fox_kernel(first_ref, q_ref, fq_ref, k_ref, fk_ref, vt_ref, o_ref, *scratch):
    nh = FOX_HEADS_PER_STEP
    qh, s_, p_, bm, m_, al, acc = (scratch[i * nh:(i + 1) * nh] for i in range(7))
    qi = pl.program_id(2)
    j0 = first_ref[(pl.program_id(0) * pl.num_programs(1) + pl.program_id(1)) * pl.num_programs(2) + qi]

    row = lax.broadcasted_iota(jnp.int32, (2 * LANES, 1), 0)
    for h in range(nh):
        pair, sub = divmod(h, 2)
        cols = slice(pair * LANES, (pair + 1) * LANES)
        qext_t = jnp.concatenate([q_ref[:, cols], fq_ref[:, cols]], axis=1).astype(jnp.float32).T
        lo = sub * FOX_HEAD_DIM
        flo = LANES + sub * F_SLOT
        own = ((row >= lo) & (row < lo + FOX_HEAD_DIM)) | ((row >= flo) & (row < flo + F_SLOT))
        qh[h][...] = jnp.where(own, qext_t, 0.0).astype(jnp.bfloat16)
        acc[h][...] = jnp.zeros_like(acc[h])
        m_[h][...] = jnp.full(m_[h].shape, NEG_BIG, jnp.float32)
        p_[h][...] = jnp.zeros_like(p_[h])
        al[h][...] = jnp.ones_like(al[h])

    def scores(j, h):
        k0 = pl.multiple_of(j * TQ, TQ)
        cols = slice((h // 2) * LANES, (h // 2 + 1) * LANES)
        kext = jnp.concatenate([k_ref[pl.ds(k0, TQ), cols], fk_ref[pl.ds(k0, TQ), cols]], axis=1)
        s = _dot(kext, qh[h][...])
        s_[h][...] = s
        bm[h][...] = jnp.max(s, axis=0, keepdims=True)

    def softmax(h, diagonal):
        s = s_[h][...]
        if diagonal:
            krow = lax.broadcasted_iota(jnp.int32, (TQ, TQ), 0)
            qcol = lax.broadcasted_iota(jnp.int32, (TQ, TQ), 1)
            s = jnp.where(krow <= qcol, s, NEG_BIG)
            blk_max = jnp.max(s, axis=0, keepdims=True)
        else:
            blk_max = bm[h][...]
        m_old = m_[h][...]
        m_new = jnp.maximum(m_old, blk_max)
        al[h][...] = jnp.exp2(m_old - m_new)
        p_[h][...] = jnp.exp2((s - m_new).astype(jnp.bfloat16))
        m_[h][...] = m_new

    def update(j, h):
        k0 = pl.multiple_of(j * TQ, TQ)
        vth = jnp.concatenate([vt_ref[h * FOX_HEAD_DIM:(h + 1) * FOX_HEAD_DIM, pl.ds(k0, TQ)],
                               jnp.ones((16, TQ), jnp.bfloat16)], axis=0)
        acc[h][...] = al[h][...] * acc[h][...] + _dot(vth, p_[h][...])

    for h in range(nh):
        scores(j0, h)

    def body(j, carry):
        prev = jnp.maximum(j - 1, j0)
        for h in range(nh):
            update(prev, h)
            softmax(h, False)
            scores(j + 1, h)
        return carry

    lax.fori_loop(j0, qi, body, 0)
    prev = jnp.maximum(qi - 1, j0)
    for h in range(nh):
        update(prev, h)
        softmax(h, True)
    for h in range(nh):
        update(qi, h)
    for pair in range(nh // 2):
        o_t = jnp.concatenate([a[0:FOX_HEAD_DIM, :] / a[FOX_HEAD_DIM:FOX_HEAD_DIM + 1, :]
                               for a in (acc[2 * pair][...], acc[2 * pair + 1][...])], axis=0)
        o_ref[:, pair * LANES:(pair + 1) * LANES] = o_t.T.astype(jnp.bfloat16)


def _first_blocks(nrm, fedge, batch, seq):
    nq = seq // TQ
    groups = FOX_HEADS // FOX_HEADS_PER_STEP
    heads = slice(0, FOX_HEADS)
    norms = jnp.sqrt(nrm.reshape(batch, nq, 2, LANES)[..., heads])
    qn, kn = norms[:, :, 0], norms[:, :, 1]
    fe = fedge.reshape(batch, nq, TQ // TM_MIX, 2, LANES)[..., FF_LANE0:FF_LANE0 + FOX_HEADS]
    f_start, f_end = fe[:, :, 0, 0], fe[:, :, -1, 1]
    bound = (qn[:, :, None] * (kn[:, None, :] + kn[:, :, None])
             + f_start[:, :, None] - f_end[:, None, :])
    needed = (~(bound <= -PRUNE_LOG2)).reshape(batch, nq, nq, groups, FOX_HEADS_PER_STEP).any(-1)
    needed = needed | jnp.eye(nq, dtype=bool)[None, :, :, None]
    first = jnp.argmax(needed, axis=2)
    return first.transpose(0, 2, 1).reshape(-1).astype(jnp.int32)


def _fox(first, q, fq, k, fk, vt, batch, seq):
    nq = seq // TQ
    nh = FOX_HEADS_PER_STEP
    width = nh * FOX_HEAD_DIM
    groups = FOX_HEADS // nh
    qspec = pl.BlockSpec((TQ, width), lambda b, g, i, first: (b * nq + i, g))
    kspec = pl.BlockSpec((seq, width), lambda b, g, i, first: (b, g))
    acc_rows = FOX_HEAD_DIM + 16
    return pl.pallas_call(
        _fox_kernel,
        grid_spec=pltpu.PrefetchScalarGridSpec(
            num_scalar_prefetch=1,
            grid=(batch, groups, nq),
            in_specs=[qspec, qspec, kspec, kspec,
                      pl.BlockSpec((None, width, seq), lambda b, g, i, first: (b, g, 0))],
            out_specs=qspec,
            scratch_shapes=([pltpu.VMEM((2 * LANES, TQ), jnp.bfloat16)] * nh
                            + [pltpu.VMEM((TQ, TQ), jnp.float32)] * nh
                            + [pltpu.VMEM((TQ, TQ), jnp.bfloat16)] * nh
                            + [pltpu.VMEM((1, TQ), jnp.float32)] * (3 * nh)
                            + [pltpu.VMEM((acc_rows, TQ), jnp.float32)] * nh)),
        out_shape=jax.ShapeDtypeStruct(q.shape, jnp.bfloat16),
        compiler_params=pltpu.CompilerParams(
            dimension_semantics=("arbitrary", "arbitrary", "arbitrary"),
            vmem_limit_bytes=VMEM_LIMIT),
        name="fox",
    )(first, q, fq, k, fk, vt)


def _outffn_kernel(x_ref, mpg_ref, yf_ref, wo_ref, ln2_ref, wgu_ref, wdn_ref, lnf_ref, o_ref, *, final):
    half = POOL_WIDTH + GLA_WIDTH
    x1 = (x_ref[...] + _dot(mpg_ref[...], wo_ref[0:half, :]) + _dot(yf_ref[...], wo_ref[half:, :]))
    ms = jnp.mean(x1 * x1, axis=-1, keepdims=True)
    h = (x1 * lax.rsqrt(ms + EPS) * ln2_ref[...]).astype(jnp.bfloat16)
    gate = _dot(h, wgu_ref[:, 0:D_FF])
    up = _dot(h, wgu_ref[:, D_FF:])
    act = (gate * jax.nn.sigmoid(gate) * up).astype(jnp.bfloat16)
    x2 = x1 + _dot(act, wdn_ref[...])
    if final:
        ms2 = jnp.mean(x2 * x2, axis=-1, keepdims=True)
        x2 = x2 * lax.rsqrt(ms2 + EPS) * lnf_ref[...]
    o_ref[...] = x2


def _outffn(x2, mpg, yf, wo, ln2, wgu, wdn, lnf, final):
    bt = x2.shape[0]
    tm = TM_FFN
    row = lambda w: pl.BlockSpec((tm, w), lambda i: (i, 0))
    single = lambda shape: pl.BlockSpec(shape, lambda i: (0, 0), pipeline_mode=pl.Buffered(1))
    return pl.pallas_call(
        functools.partial(_outffn_kernel, final=final),
        grid=(bt // tm,),
        in_specs=[row(D_MODEL), row(POOL_WIDTH + GLA_WIDTH), row(FOX_WIDTH),
                  single(wo.shape), _const_spec((1, D_MODEL)), single(wgu.shape), single(wdn.shape),
                  _const_spec((1, D_MODEL))],
        out_specs=row(D_MODEL),
        out_shape=jax.ShapeDtypeStruct((bt, D_MODEL), jnp.float32),
        compiler_params=pltpu.CompilerParams(dimension_semantics=("arbitrary",),
                                             vmem_limit_bytes=VMEM_LIMIT),
        name="outffn",
    )(x2, mpg, yf, wo, ln2, wgu, wdn, lnf)


def _placement(dtype=jnp.bfloat16):
    pq = np.zeros((3, LANES, FOX_WIDTH), np.float32)
    pk = np.zeros((3, LANES, FOX_WIDTH), np.float32)
    cq = np.zeros((1, FOX_WIDTH), np.float32)
    ck = np.zeros((1, FOX_WIDTH), np.float32)
    for h in range(FOX_HEADS):
        base = (h // 2) * LANES + (h % 2) * F_SLOT
        for part in range(3):
            pq[part, FF_LANE0 + h, base + part] = 1.0
            pk[part, FF_LANE0 + h, base + 3 + part] = -1.0
            cq[0, base + 3 + part] = 1.0
            ck[0, base + part] = 1.0
    return jnp.asarray(pq, dtype), jnp.asarray(cq), jnp.asarray(pk, dtype), jnp.asarray(ck)


def _triangles(dtype=jnp.bfloat16):
    r = np.arange(TM_MIX)[:, None]
    c = np.arange(TM_MIX)[None, :]
    full = (r >= c)
    chunk = full & (r // CHUNK == c // CHUNK)
    return jnp.asarray(chunk, dtype), jnp.asarray(full, dtype)


@jax.jit
def kernel(x, ln1, w_in, w_pool, pool_scale, w_a_up, b_a, gla_gn, b_f, w_o, ln2, w_gu, w_down, ln_f):
    batch, seq, _ = x.shape
    depth = w_in.shape[0]
    bf16 = jnp.bfloat16
    xf = x.reshape(batch * seq, D_MODEL)

    tri_chunk, tri_full = _triangles()
    pq, cq, pk, ck = _placement()
    group_of = np.arange(POOL_WIDTH) // POOL_GROUP
    same_group = jnp.asarray(group_of[:, None] == group_of[None, :])
    head_sum = jnp.asarray(np.arange(FOX_WIDTH)[:, None] // FOX_HEAD_DIM == np.arange(LANES)[None, :], bf16)
    assert TM_IN == TQ and TQ % TM_MIX == 0

    o0 = POOL_WIDTH
    o1 = o0 + 4 * GLA_WIDTH
    o2 = o1 + GLA_GATE_RANK
    o3 = o2 + 3 * FOX_WIDTH
    col_scale = np.ones((o3 + FOX_HEADS,), np.float32)
    col_scale[o0:o0 + GLA_WIDTH] = GLA_HEAD_DIM ** -0.5
    col_scale[o2:o2 + FOX_WIDTH] = FOX_HEAD_DIM ** -0.5 * LOG2E
    for l in range(depth):
        w = (w_in[l] * col_scale).astype(bf16)
        wu = w[:, 0:o0]
        wg = w[:, o0:o1]
        ws = jnp.pad(jnp.concatenate([w[:, o1:o2], w[:, o3:o3 + FOX_HEADS]], axis=1),
                     ((0, 0), (0, LANES - GLA_GATE_RANK - FOX_HEADS)))
        wq = w[:, o2:o2 + FOX_WIDTH]
        wk = w[:, o2 + FOX_WIDTH:o2 + 2 * FOX_WIDTH]
        wvt = w[:, o2 + 2 * FOX_WIDTH:o3].T
        u, gla, small, q, k, vt, nrm = _inproj(xf, ln1[l][None, :], wu, wg, ws, wq, wk, wvt, head_sum,
                                               batch, seq)

        wp = w_pool[l].reshape(POOL_WIDTH, POOL_GROUP)
        wpool_bd = jnp.where(same_group, jnp.tile(wp, (1, len(POOL_WINDOWS))), 0.0).astype(bf16)
        wa = jnp.pad(w_a_up[l], ((0, LANES - GLA_GATE_RANK), (0, 0)))
        wa_hi = wa.astype(bf16)
        wa_lo = (wa - wa_hi.astype(jnp.float32)).astype(bf16)
        bf_row = jnp.pad(b_f[l][None, :], ((0, 0), (FF_LANE0, LANES - FF_LANE0 - FOX_HEADS)))
        consts = [wpool_bd, pool_scale[l][None, :], wa_hi, wa_lo, b_a[l][None, :],
                  gla_gn[l][None, :], bf_row, tri_chunk, tri_full, pq, cq, pk, ck]
        mpg, fq, fk, fedge = _mixpg(u, gla, small, consts, batch, seq)

        yf = _fox(_first_blocks(nrm, fedge, batch, seq), q, fq, k, fk, vt, batch, seq)

        xf = _outffn(xf, mpg, yf, w_o[l].astype(bf16), ln2[l][None, :], w_gu[l].astype(bf16),
                     w_down[l].astype(bf16), ln_f[None, :], final=(l == depth - 1))
    return xf.reshape(batch, seq, D_MODEL)
```

```python
import functools

import jax
import jax.numpy as jnp
import numpy as np
from jax import lax
from jax.experimental import pallas as pl
from jax.experimental.pallas import tpu as pltpu

D_MODEL = 1024
CHUNK = 64
EPS = 1e-6
POOL_WIDTH = 256
POOL_WINDOWS = (2, 4, 8, 16)
POOL_GROUP = 64
GLA_WIDTH = 256
GLA_HEADS = 4
GLA_HEAD_DIM = 64
GLA_GATE_RANK = 16
GLA_TAU = 16.0
FOX_WIDTH = 512
FOX_HEAD_DIM = 64
FOX_HEADS = 8
D_FF = 2816

LANES = 128
HALO = 32
FF_LANE0 = GLA_GATE_RANK
F_SLOT = 8
NEG_BIG = -1e30
LOG2E = 1.4426950408889634
PRUNE_LOG2 = 150.0
NORM_SLACK = 1.01

VMEM_LIMIT = 56 * 1024 * 1024

TM_IN = 512
TM_MIX = 256
MIX_SUBTILES = 4
TQ = 512
FOX_HEADS_PER_STEP = 4
TM_FFN = 512


def _dot(a, b):
    return jnp.dot(a, b, preferred_element_type=jnp.float32)


def _dot_nt(a, b):
    return lax.dot_general(a, b, (((1,), (1,)), ((), ())), preferred_element_type=jnp.float32)


def _dot_tn(a, b):
    return lax.dot_general(a, b, (((0,), (0,)), ((), ())), preferred_element_type=jnp.float32)


def _split3(x):
    hi = x.astype(jnp.bfloat16)
    r = x - hi.astype(jnp.float32)
    mid = r.astype(jnp.bfloat16)
    lo = (r - mid.astype(jnp.float32)).astype(jnp.bfloat16)
    return hi, mid, lo


def _log_sigmoid(x):
    return jnp.minimum(x, 0.0) - jnp.log(1.0 + jnp.exp(-jnp.abs(x)))


def _const_spec(shape):
    nd = len(shape)
    return pl.BlockSpec(shape, lambda *_: (0,) * nd)


def _inproj_kernel(x_ref, ln_ref, wu_ref, wg_ref, ws_ref, wq_ref, wk_ref, wvt_ref, hsum_ref,
                   u_ref, gla_ref, small_ref, q_ref, k_ref, vt_ref, nrm_ref):
    x = x_ref[...]
    ms = jnp.mean(x * x, axis=-1, keepdims=True)
    h = (x * lax.rsqrt(ms + EPS) * ln_ref[...]).astype(jnp.bfloat16)
    u_ref[...] = _dot(h, wu_ref[...])
    gla_ref[...] = _dot(h, wg_ref[...]).astype(jnp.bfloat16)
    small_ref[...] = _dot(h, ws_ref[...])
    qb = _dot(h, wq_ref[...]).astype(jnp.bfloat16)
    kb = _dot(h, wk_ref[...]).astype(jnp.bfloat16)
    q_ref[...] = qb
    k_ref[...] = kb
    vt_ref[...] = _dot_nt(wvt_ref[...], h).astype(jnp.bfloat16)

    def max_sq_norm(zb):
        z = zb.astype(jnp.float32)
        n2 = _dot((z * z).astype(jnp.bfloat16), hsum_ref[...])
        return jnp.max(n2, axis=0, keepdims=True) * NORM_SLACK

    nrm_ref[...] = jnp.concatenate([max_sq_norm(qb), max_sq_norm(kb)], axis=0)


def _inproj(x2, ln, wu, wg, ws, wq, wk, wvt, hsum, batch, seq):
    bt = x2.shape[0]
    tm = TM_IN
    nt = seq // tm
    row = lambda w: pl.BlockSpec((tm, w), lambda i: (i, 0))
    return pl.pallas_call(
        _inproj_kernel,
        grid=(bt // tm,),
        in_specs=[row(D_MODEL), _const_spec((1, D_MODEL)),
                  _const_spec(wu.shape), _const_spec(wg.shape), _const_spec(ws.shape),
                  _const_spec(wq.shape), _const_spec(wk.shape), _const_spec(wvt.shape),
                  _const_spec(hsum.shape)],
        out_specs=[row(POOL_WIDTH), row(4 * GLA_WIDTH), row(LANES), row(FOX_WIDTH), row(FOX_WIDTH),
                   pl.BlockSpec((None, FOX_WIDTH, tm), lambda i: (i // nt, 0, i % nt)),
                   pl.BlockSpec((None, 2, LANES), lambda i: (i, 0, 0))],
        out_shape=[jax.ShapeDtypeStruct((bt, POOL_WIDTH), jnp.float32),
                   jax.ShapeDtypeStruct((bt, 4 * GLA_WIDTH), jnp.bfloat16),
                   jax.ShapeDtypeStruct((bt, LANES), jnp.float32),
                   jax.ShapeDtypeStruct((bt, FOX_WIDTH), jnp.bfloat16),
                   jax.ShapeDtypeStruct((bt, FOX_WIDTH), jnp.bfloat16),
                   jax.ShapeDtypeStruct((batch, FOX_WIDTH, seq), jnp.bfloat16),
                   jax.ShapeDtypeStruct((bt // tm, 2, LANES), jnp.float32)],
        compiler_params=pltpu.CompilerParams(dimension_semantics=("arbitrary",),
                                             vmem_limit_bytes=VMEM_LIMIT),
        name="inproj",
    )(x2, ln, wu, wg, ws, wq, wk, wvt, hsum)


def _mixpg_kernel(u_ref, gla_ref, small_ref,
                  wpool_ref, pscale_ref, wa_hi_ref, wa_lo_ref, ba_ref, gn_ref, bf_ref,
                  tri_chunk_ref, tri_full_ref, pq_ref, cq_ref, pk_ref, ck_ref, hmean_ref,
                  mpg_ref, fq_ref, fk_ref, fedge_ref,
                  ubuf, fcarry, st_ref, *bufs):
    ti = pl.program_id(1)
    assert POOL_WINDOWS == tuple(2 ** (n + 1) for n in range(len(POOL_WINDOWS)))
    assert HALO == 8 * len(POOL_WINDOWS)

    @pl.when(ti == 0)
    def _():
        ubuf[0:HALO, :] = jnp.zeros((HALO, POOL_WIDTH), jnp.float32)
        fcarry[...] = jnp.zeros_like(fcarry)
        st_ref[...] = jnp.zeros_like(st_ref)

    refs = (u_ref, gla_ref, small_ref, wpool_ref, pscale_ref, wa_hi_ref, wa_lo_ref, ba_ref, gn_ref, bf_ref,
            tri_chunk_ref, tri_full_ref, pq_ref, cq_ref, pk_ref, ck_ref, hmean_ref,
            mpg_ref, fq_ref, fk_ref, fedge_ref, ubuf, fcarry, st_ref)
    tiles = [_mix_tile(ti * MIX_SUBTILES + sub, sub, *refs, bufs[sub], bufs[MIX_SUBTILES:])
             for sub in range(MIX_SUBTILES)]
    done = object()
    while tiles:
        tiles = [t for t in tiles if next(t, done) is not done]


def _mix_tile(tile_idx, sub, u_ref, gla_ref, small_ref,
              wpool_ref, pscale_ref, wa_hi_ref, wa_lo_ref, ba_ref, gn_ref, bf_ref,
              tri_chunk_ref, tri_full_ref, pq_ref, cq_ref, pk_ref, ck_ref, hmean_ref,
              mpg_ref, fq_ref, fk_ref, fedge_ref, ubuf, fcarry, st_ref, bcum_ref, pool_bufs):
    tm = TM_MIX
    rows = slice(sub * tm, (sub + 1) * tm)

    u = u_ref[rows, :]
    ubuf[HALO:HALO + tm, :] = u
    total = HALO + tm
    prev = ubuf
    sums = {}
    for n, w in enumerate(POOL_WINDOWS):
        shift, start = w // 2, 8 * (n + 1)
        val = prev[start:total, :] + prev[start - shift:total - shift, :]
        sums[w] = val[HALO - start:, :]
        if n + 1 < len(POOL_WINDOWS):
            pool_bufs[n][start:total, :] = val
            prev = pool_bufs[n]
    lane = lax.broadcasted_iota(jnp.int32, (tm, POOL_WIDTH), 1)
    tpos = tile_idx * tm + lax.broadcasted_iota(jnp.int32, (tm, POOL_WIDTH), 0)
    wsum = sums[POOL_WINDOWS[-1]]
    width = jnp.full((tm, POOL_WIDTH), POOL_WINDOWS[-1], jnp.int32)
    for gi in range(len(POOL_WINDOWS) - 2, -1, -1):
        in_group = lane < (gi + 1) * POOL_GROUP
        wsum = jnp.where(in_group, sums[POOL_WINDOWS[gi]], wsum)
        width = jnp.where(in_group, POOL_WINDOWS[gi], width)
    cnt = jnp.minimum(tpos + 1, width).astype(jnp.float32)
    d = wsum / cnt - u
    y_pool = _dot(d.astype(jnp.bfloat16), wpool_ref[...]) * pscale_ref[...]
    mpg_ref[rows, 0:POOL_WIDTH] = y_pool.astype(jnp.bfloat16)
    ubuf[0:HALO, :] = ubuf[tm:tm + HALO, :]
    yield

    small = small_ref[rows, :]
    s_hi = small.astype(jnp.bfloat16)
    s_lo = (small - s_hi.astype(jnp.float32)).astype(jnp.bfloat16)
    a = (_dot(s_hi, wa_hi_ref[...]) + _dot(s_lo, wa_hi_ref[...]) + _dot(s_hi, wa_lo_ref[...])
         + ba_ref[...])
    la = _log_sigmoid(a) / GLA_TAU
    la_hi, la_mid, la_lo = _split3(la)
    tri_c = tri_chunk_ref[...]
    bcum_ref[...] = _dot(tri_c, la_hi) + _dot(tri_c, la_mid) + _dot(tri_c, la_lo)
    yield

    slane = lax.broadcasted_iota(jnp.int32, (tm, LANES), 1)
    is_ff = (slane >= FF_LANE0) & (slane < FF_LANE0 + FOX_HEADS)
    lf = jnp.where(is_ff, _log_sigmoid(small + bf_ref[...]), 0.0)
    lf_hi, lf_mid, lf_lo = _split3(lf)
    tri_f = tri_full_ref[...]
    fcum = _dot(tri_f, lf_hi) + _dot(tri_f, lf_mid) + _dot(tri_f, lf_lo) + fcarry[...]
    fcarry[...] = fcum[tm - 1:tm, :]
    fl = fcum * LOG2E
    fedge_ref[sub] = jnp.concatenate([fl[0:1, :], fl[tm - 1:tm, :]], axis=0)
    f_hi, f_mid, f_lo = _split3(fl)
    fq_ref[rows, :] = (_dot(f_hi, pq_ref[0]) + _dot(f_mid, pq_ref[1]) + _dot(f_lo, pq_ref[2])
                       + cq_ref[...]).astype(jnp.bfloat16)
    fk_ref[rows, :] = (_dot(f_hi, pk_ref[0]) + _dot(f_mid, pk_ref[1]) + _dot(f_lo, pk_ref[2])
                       + ck_ref[...]).astype(jnp.bfloat16)
    yield

    glane = lax.broadcasted_iota(jnp.int32, (1, GLA_WIDTH), 1)
    head_masks = [(glane >= h * GLA_HEAD_DIM) & (glane < (h + 1) * GLA_HEAD_DIM)
                  for h in range(GLA_HEADS)]
    r_i = lax.broadcasted_iota(jnp.int32, (GLA_WIDTH, GLA_WIDTH), 0) // GLA_HEAD_DIM
    c_i = lax.broadcasted_iota(jnp.int32, (GLA_WIDTH, GLA_WIDTH), 1) // GLA_HEAD_DIM
    same_head = r_i == c_i
    n_chunks = tm // CHUNK

    def chunk_row(offset):
        return jnp.concatenate(
            [jnp.broadcast_to(bcum_ref[c * CHUNK + offset:c * CHUNK + offset + 1, :], (CHUNK, GLA_WIDTH))
             for c in range(n_chunks)], axis=0)

    q = gla_ref[rows, 0:GLA_WIDTH].astype(jnp.float32)
    k = gla_ref[rows, GLA_WIDTH:2 * GLA_WIDTH].astype(jnp.float32)
    v = gla_ref[rows, 2 * GLA_WIDTH:3 * GLA_WIDTH]
    bc = bcum_ref[...]
    b_mid = chunk_row(CHUNK // 2 - 1)
    b_last = chunk_row(CHUNK - 1)
    q_in = (q * jnp.exp(bc - b_mid)).astype(jnp.bfloat16)
    k_in = (k * jnp.exp(b_mid - bc)).astype(jnp.bfloat16)
    k_kv = (k * jnp.exp(b_last - bc)).astype(jnp.bfloat16)
    q_st = (q * jnp.exp(bc)).astype(jnp.bfloat16)
    yield

    in_chunk_causal = tri_c > 0
    att = jnp.concatenate(
        [jnp.where(in_chunk_causal,
                   _dot_nt(jnp.where(head_masks[h], q_in, jnp.zeros_like(q_in)), k_in),
                   0.0).astype(jnp.bfloat16) for h in range(GLA_HEADS)], axis=1)
    v_heads = jnp.concatenate([jnp.where(head_masks[h], v, jnp.zeros_like(v))
                               for h in range(GLA_HEADS)], axis=0)
    o_intra = _dot(att, v_heads)
    chunk_rows = [slice(c * CHUNK, (c + 1) * CHUNK) for c in range(n_chunks)]
    kv_t = [jnp.where(same_head, _dot_tn(v[cr], k_kv[cr]), 0.0) for cr in chunk_rows]
    yield

    st = st_ref[...]
    o_inter = []
    for c, cr in enumerate(chunk_rows):
        o_inter.append(_dot_nt(q_st[cr], st.astype(jnp.bfloat16)))
        dec = jnp.exp(bcum_ref[(c + 1) * CHUNK - 1:(c + 1) * CHUNK, :])
        st = st * dec + kv_t[c]
    st_ref[...] = st
    yield

    o = o_intra + jnp.concatenate(o_inter, axis=0)
    o2 = o * o
    o2_hi = o2.astype(jnp.bfloat16)
    o2_lo = (o2 - o2_hi.astype(jnp.float32)).astype(jnp.bfloat16)
    mean_sq = _dot(o2_hi, hmean_ref[...]) + _dot(o2_lo, hmean_ref[...])
    g = gla_ref[rows, 3 * GLA_WIDTH:4 * GLA_WIDTH].astype(jnp.float32)
    y_gla = o * lax.rsqrt(mean_sq + EPS) * gn_ref[...] * (g * jax.nn.sigmoid(g))
    mpg_ref[rows, POOL_WIDTH:POOL_WIDTH + GLA_WIDTH] = y_gla.astype(jnp.bfloat16)


def _mixpg(u, gla, small, consts, batch, seq):
    tm = TM_MIX
    step = MIX_SUBTILES * tm
    nt = seq // step
    row = lambda w: pl.BlockSpec((step, w), lambda b, i: (b * nt + i, 0))
    bt = u.shape[0]
    return pl.pallas_call(
        _mixpg_kernel,
        grid=(batch, nt),
        in_specs=[row(POOL_WIDTH), row(4 * GLA_WIDTH), row(LANES)] + [_const_spec(c.shape) for c in consts],
        out_specs=[row(POOL_WIDTH + GLA_WIDTH), row(FOX_WIDTH), row(FOX_WIDTH),
                   pl.BlockSpec((MIX_SUBTILES, 2, LANES), lambda b, i: (b * nt + i, 0, 0))],
        out_shape=[jax.ShapeDtypeStruct((bt, POOL_WIDTH + GLA_WIDTH), jnp.bfloat16),
                   jax.ShapeDtypeStruct((bt, FOX_WIDTH), jnp.bfloat16),
                   jax.ShapeDtypeStruct((bt, FOX_WIDTH), jnp.bfloat16),
                   jax.ShapeDtypeStruct((bt // tm, 2, LANES), jnp.float32)],
        scratch_shapes=[pltpu.VMEM((tm + HALO, POOL_WIDTH), jnp.float32),
                        pltpu.VMEM((1, LANES), jnp.float32),
                        pltpu.VMEM((GLA_WIDTH, GLA_WIDTH), jnp.float32)]
                       + [pltpu.VMEM((tm, GLA_WIDTH), jnp.float32)] * MIX_SUBTILES
                       + [pltpu.VMEM((tm + HALO, POOL_WIDTH), jnp.float32)] * (len(POOL_WINDOWS) - 1),
        compiler_params=pltpu.CompilerParams(dimension_semantics=("arbitrary", "arbitrary"),
                                             vmem_limit_bytes=VMEM_LIMIT),
        name="mixpg",
    )(u, gla, small, *consts)


def _fox_kernel(first_ref, q_ref, fq_ref, k_ref, fk_ref, vt_ref, o_ref, *scratch):
    nh = FOX_HEADS_PER_STEP
    qh, s_, p_, bm, m_, al, acc = (scratch[i * nh:(i + 1) * nh] for i in range(7))
    qi = pl.program_id(2)
    j0 = first_ref[(pl.program_id(0) * pl.num_programs(1) + pl.program_id(1)) * pl.num_programs(2) + qi]

    row = lax.broadcasted_iota(jnp.int32, (2 * LANES, 1), 0)
    for h in range(nh):
        pair, sub = divmod(h, 2)
        cols = slice(pair * LANES, (pair + 1) * LANES)
        qext_t = jnp.concatenate([q_ref[:, cols], fq_ref[:, cols]], axis=1).astype(jnp.float32).T
        lo = sub * FOX_HEAD_DIM
        flo = LANES + sub * F_SLOT
        own = ((row >= lo) & (row < lo + FOX_HEAD_DIM)) | ((row >= flo) & (row < flo + F_SLOT))
        qh[h][...] = jnp.where(own, qext_t, 0.0).astype(jnp.bfloat16)
        acc[h][...] = jnp.zeros_like(acc[h])
        m_[h][...] = jnp.full(m_[h].shape, NEG_BIG, jnp.float32)
        p_[h][...] = jnp.zeros_like(p_[h])
        al[h][...] = jnp.ones_like(al[h])

    def scores(j, h):
        k0 = pl.multiple_of(j * TQ, TQ)
        cols = slice((h // 2) * LANES, (h // 2 + 1) * LANES)
        kext = jnp.concatenate([k_ref[pl.ds(k0, TQ), cols], fk_ref[pl.ds(k0, TQ), cols]], axis=1)
        s = _dot(kext, qh[h][...])
        s_[h][...] = s
        bm[h][...] = jnp.max(s, axis=0, keepdims=True)

    def softmax(h, diagonal):
        s = s_[h][...]
        if diagonal:
            krow = lax.broadcasted_iota(jnp.int32, (TQ, TQ), 0)
            qcol = lax.broadcasted_iota(jnp.int32, (TQ, TQ), 1)
            s = jnp.where(krow <= qcol, s, NEG_BIG)
            blk_max = jnp.max(s, axis=0, keepdims=True)
        else:
            blk_max = bm[h][...]
        m_old = m_[h][...]
        m_new = jnp.maximum(m_old, blk_max)
        al[h][...] = jnp.exp2(m_old - m_new)
        p_[h][...] = jnp.exp2((s - m_new).astype(jnp.bfloat16))
        m_[h][...] = m_new

    def update(j, h):
        k0 = pl.multiple_of(j * TQ, TQ)
        vth = jnp.concatenate([vt_ref[h * FOX_HEAD_DIM:(h + 1) * FOX_HEAD_DIM, pl.ds(k0, TQ)],
                               jnp.ones((16, TQ), jnp.bfloat16)], axis=0)
        acc[h][...] = al[h][...] * acc[h][...] + _dot(vth, p_[h][...])

    for h in range(nh):
        scores(j0, h)

    def body(j, carry):
        prev = jnp.maximum(j - 1, j0)
        for h in range(nh):
            update(prev, h)
            softmax(h, False)
            scores(j + 1, h)
        return carry

    lax.fori_loop(j0, qi, body, 0)
    prev = jnp.maximum(qi - 1, j0)
    for h in range(nh):
        update(prev, h)
        softmax(h, True)
    for h in range(nh):
        update(qi, h)
    for pair in range(nh // 2):
        o_t = jnp.concatenate([a[0:FOX_HEAD_DIM, :] / a[FOX_HEAD_DIM:FOX_HEAD_DIM + 1, :]
                               for a in (acc[2 * pair][...], acc[2 * pair + 1][...])], axis=0)
        o_ref[:, pair * LANES:(pair + 1) * LANES] = o_t.T.astype(jnp.bfloat16)


def _first_blocks(nrm, fedge, batch, seq):
    nq = seq // TQ
    groups = FOX_HEADS // FOX_HEADS_PER_STEP
    heads = slice(0, FOX_HEADS)
    norms = jnp.sqrt(nrm.reshape(batch, nq, 2, LANES)[..., heads])
    qn, kn = norms[:, :, 0], norms[:, :, 1]
    fe = fedge.reshape(batch, nq, TQ // TM_MIX, 2, LANES)[..., FF_LANE0:FF_LANE0 + FOX_HEADS]
    f_start, f_end = fe[:, :, 0, 0], fe[:, :, -1, 1]
    bound = (qn[:, :, None] * (kn[:, None, :] + kn[:, :, None])
             + f_start[:, :, None] - f_end[:, None, :])
    j_idx = lax.broadcasted_iota(jnp.int32, bound.shape, 2)
    i_idx = lax.broadcasted_iota(jnp.int32, bound.shape, 1)
    skippable = (bound <= -PRUNE_LOG2) & (j_idx != i_idx)
    first_h = jnp.min(jnp.where(skippable, nq, j_idx), axis=2)
    first = jnp.min(first_h.reshape(batch, nq, groups, FOX_HEADS_PER_STEP), axis=-1)
    return first.transpose(0, 2, 1).reshape(-1)


def _fox(first, q, fq, k, fk, vt, batch, seq):
    nq = seq // TQ
    nh = FOX_HEADS_PER_STEP
    width = nh * FOX_HEAD_DIM
    groups = FOX_HEADS // nh
    qspec = pl.BlockSpec((TQ, width), lambda b, g, i, first: (b * nq + i, g))
    kspec = pl.BlockSpec((seq, width), lambda b, g, i, first: (b, g))
    acc_rows = FOX_HEAD_DIM + 16
    return pl.pallas_call(
        _fox_kernel,
        grid_spec=pltpu.PrefetchScalarGridSpec(
            num_scalar_prefetch=1,
            grid=(batch, groups, nq),
            in_specs=[qspec, qspec, kspec, kspec,
                      pl.BlockSpec((None, width, seq), lambda b, g, i, first: (b, g, 0))],
            out_specs=qspec,
            scratch_shapes=([pltpu.VMEM((2 * LANES, TQ), jnp.bfloat16)] * nh
                            + [pltpu.VMEM((TQ, TQ), jnp.float32)] * nh
                            + [pltpu.VMEM((TQ, TQ), jnp.bfloat16)] * nh
                            + [pltpu.VMEM((1, TQ), jnp.float32)] * (3 * nh)
                            + [pltpu.VMEM((acc_rows, TQ), jnp.float32)] * nh)),
        out_shape=jax.ShapeDtypeStruct(q.shape, jnp.bfloat16),
        compiler_params=pltpu.CompilerParams(
            dimension_semantics=("arbitrary", "arbitrary", "arbitrary"),
            vmem_limit_bytes=VMEM_LIMIT),
        name="fox",
    )(first, q, fq, k, fk, vt)


def _outffn_kernel(x_ref, mpg_ref, yf_ref, wo_ref, ln2_ref, wgu_ref, wdn_ref, lnf_ref, o_ref, *, final):
    half = POOL_WIDTH + GLA_WIDTH
    x1 = (x_ref[...] + _dot(mpg_ref[...], wo_ref[0:half, :]) + _dot(yf_ref[...], wo_ref[half:, :]))
    ms = jnp.mean(x1 * x1, axis=-1, keepdims=True)
    h = (x1 * lax.rsqrt(ms + EPS) * ln2_ref[...]).astype(jnp.bfloat16)
    gate = _dot(h, wgu_ref[:, 0:D_FF])
    up = _dot(h, wgu_ref[:, D_FF:])
    act = (gate * jax.nn.sigmoid(gate) * up).astype(jnp.bfloat16)
    x2 = x1 + _dot(act, wdn_ref[...])
    if final:
        ms2 = jnp.mean(x2 * x2, axis=-1, keepdims=True)
        x2 = x2 * lax.rsqrt(ms2 + EPS) * lnf_ref[...]
    o_ref[...] = x2


def _outffn(x2, mpg, yf, wo, ln2, wgu, wdn, lnf, final):
    bt = x2.shape[0]
    tm = TM_FFN
    row = lambda w: pl.BlockSpec((tm, w), lambda i: (i, 0))
    single = lambda shape: pl.BlockSpec(shape, lambda i: (0, 0), pipeline_mode=pl.Buffered(1))
    return pl.pallas_call(
        functools.partial(_outffn_kernel, final=final),
        grid=(bt // tm,),
        in_specs=[row(D_MODEL), row(POOL_WIDTH + GLA_WIDTH), row(FOX_WIDTH),
                  single(wo.shape), _const_spec((1, D_MODEL)), single(wgu.shape), single(wdn.shape),
                  _const_spec((1, D_MODEL))],
        out_specs=row(D_MODEL),
        out_shape=jax.ShapeDtypeStruct((bt, D_MODEL), jnp.float32),
        compiler_params=pltpu.CompilerParams(dimension_semantics=("arbitrary",),
                                             vmem_limit_bytes=VMEM_LIMIT),
        name="outffn",
    )(x2, mpg, yf, wo, ln2, wgu, wdn, lnf)


def _placement(dtype=jnp.bfloat16):
    pq = np.zeros((3, LANES, FOX_WIDTH), np.float32)
    pk = np.zeros((3, LANES, FOX_WIDTH), np.float32)
    cq = np.zeros((1, FOX_WIDTH), np.float32)
    ck = np.zeros((1, FOX_WIDTH), np.float32)
    for h in range(FOX_HEADS):
        base = (h // 2) * LANES + (h % 2) * F_SLOT
        for part in range(3):
            pq[part, FF_LANE0 + h, base + part] = 1.0
            pk[part, FF_LANE0 + h, base + 3 + part] = -1.0
            cq[0, base + 3 + part] = 1.0
            ck[0, base + part] = 1.0
    return jnp.asarray(pq, dtype), jnp.asarray(cq), jnp.asarray(pk, dtype), jnp.asarray(ck)


def _triangles(dtype=jnp.bfloat16):
    r = np.arange(TM_MIX)[:, None]
    c = np.arange(TM_MIX)[None, :]
    full = (r >= c)
    chunk = full & (r // CHUNK == c // CHUNK)
    return jnp.asarray(chunk, dtype), jnp.asarray(full, dtype)


@jax.jit
def kernel(x, ln1, w_in, w_pool, pool_scale, w_a_up, b_a, gla_gn, b_f, w_o, ln2, w_gu, w_down, ln_f):
    batch, seq, _ = x.shape
    depth = w_in.shape[0]
    bf16 = jnp.bfloat16
    xf = x.reshape(batch * seq, D_MODEL)

    tri_chunk, tri_full = _triangles()
    pq, cq, pk, ck = _placement()
    group_of = np.arange(POOL_WIDTH) // POOL_GROUP
    same_group = jnp.asarray(group_of[:, None] == group_of[None, :])
    head_sum = jnp.asarray(np.arange(FOX_WIDTH)[:, None] // FOX_HEAD_DIM == np.arange(LANES)[None, :], bf16)
    assert TM_IN == TQ and TQ % TM_MIX == 0
    gla_head = np.arange(GLA_WIDTH) // GLA_HEAD_DIM
    head_mean = jnp.asarray((gla_head[:, None] == gla_head[None, :]) / GLA_HEAD_DIM, bf16)

    o0 = POOL_WIDTH
    o1 = o0 + 4 * GLA_WIDTH
    o2 = o1 + GLA_GATE_RANK
    o3 = o2 + 3 * FOX_WIDTH
    col_scale = np.ones((o3 + FOX_HEADS,), np.float32)
    col_scale[o0:o0 + GLA_WIDTH] = GLA_HEAD_DIM ** -0.5
    col_scale[o2:o2 + FOX_WIDTH] = FOX_HEAD_DIM ** -0.5 * LOG2E
    for l in range(depth):
        w = (w_in[l] * col_scale).astype(bf16)
        wu = w[:, 0:o0]
        wg = w[:, o0:o1]
        ws = jnp.pad(jnp.concatenate([w[:, o1:o2], w[:, o3:o3 + FOX_HEADS]], axis=1),
                     ((0, 0), (0, LANES - GLA_GATE_RANK - FOX_HEADS)))
        wq = w[:, o2:o2 + FOX_WIDTH]
        wk = w[:, o2 + FOX_WIDTH:o2 + 2 * FOX_WIDTH]
        wvt = w[:, o2 + 2 * FOX_WIDTH:o3].T
        u, gla, small, q, k, vt, nrm = _inproj(xf, ln1[l][None, :], wu, wg, ws, wq, wk, wvt, head_sum,
                                               batch, seq)

        wp = w_pool[l].reshape(POOL_WIDTH, POOL_GROUP)
        wpool_bd = jnp.where(same_group, jnp.tile(wp, (1, len(POOL_WINDOWS))), 0.0).astype(bf16)
        wa = jnp.pad(w_a_up[l], ((0, LANES - GLA_GATE_RANK), (0, 0)))
        wa_hi = wa.astype(bf16)
        wa_lo = (wa - wa_hi.astype(jnp.float32)).astype(bf16)
        bf_row = jnp.pad(b_f[l][None, :], ((0, 0), (FF_LANE0, LANES - FF_LANE0 - FOX_HEADS)))
        consts = [wpool_bd, pool_scale[l][None, :], wa_hi, wa_lo, b_a[l][None, :],
                  gla_gn[l][None, :], bf_row, tri_chunk, tri_full, pq, cq, pk, ck, head_mean]
        mpg, fq, fk, fedge = _mixpg(u, gla, small, consts, batch, seq)

        yf = _fox(_first_blocks(nrm, fedge, batch, seq), q, fq, k, fk, vt, batch, seq)

        xf = _outffn(xf, mpg, yf, w_o[l].astype(bf16), ln2[l][None, :], w_gu[l].astype(bf16),
                     w_down[l].astype(bf16), ln_f[None, :], final=(l == depth - 1))
    return xf.reshape(batch, seq, D_MODEL)
```

```python
import functools

import jax
import jax.numpy as jnp
import numpy as np
from jax import lax
from jax.experimental import pallas as pl
from jax.experimental.pallas import tpu as pltpu

D_MODEL = 1024
CHUNK = 64
EPS = 1e-6
POOL_WIDTH = 256
POOL_WINDOWS = (2, 4, 8, 16)
POOL_GROUP = 64
GLA_WIDTH = 256
GLA_HEADS = 4
GLA_HEAD_DIM = 64
GLA_GATE_RANK = 16
GLA_TAU = 16.0
FOX_WIDTH = 512
FOX_HEAD_DIM = 64
FOX_HEADS = 8
D_FF = 2816

LANES = 128
HALO = 32
FF_LANE0 = GLA_GATE_RANK
F_SLOT = 8
NEG_BIG = -1e30
LOG2E = 1.4426950408889634
PRUNE_LOG2 = 150.0
NORM_SLACK = 1.01

VMEM_LIMIT = 56 * 1024 * 1024

TM_IN = 512
TM_MIX = 256
MIX_SUBTILES = 8
TQ = 512
FOX_HEADS_PER_STEP = 4
TM_FFN = 512


def _dot(a, b):
    return jnp.dot(a, b, preferred_element_type=jnp.float32)


def _dot_nt(a, b):
    return lax.dot_general(a, b, (((1,), (1,)), ((), ())), preferred_element_type=jnp.float32)


def _dot_tn(a, b):
    return lax.dot_general(a, b, (((0,), (0,)), ((), ())), preferred_element_type=jnp.float32)


def _split3(x):
    hi = x.astype(jnp.bfloat16)
    r = x - hi.astype(jnp.float32)
    mid = r.astype(jnp.bfloat16)
    lo = (r - mid.astype(jnp.float32)).astype(jnp.bfloat16)
    return hi, mid, lo


def _log_sigmoid(x):
    return jnp.minimum(x, 0.0) - jnp.log(1.0 + jnp.exp(-jnp.abs(x)))


def _const_spec(shape):
    nd = len(shape)
    return pl.BlockSpec(shape, lambda *_: (0,) * nd)


def _inproj_kernel(x_ref, ln_ref, wu_ref, wg_ref, ws_ref, wq_ref, wk_ref, wvt_ref, hsum_ref,
                   u_ref, gla_ref, small_ref, q_ref, k_ref, vt_ref, nrm_ref):
    x = x_ref[...]
    ms = jnp.mean(x * x, axis=-1, keepdims=True)
    h = (x * lax.rsqrt(ms + EPS) * ln_ref[...]).astype(jnp.bfloat16)
    u_ref[...] = _dot(h, wu_ref[...])
    gla_ref[...] = _dot(h, wg_ref[...]).astype(jnp.bfloat16)
    small_ref[...] = _dot(h, ws_ref[...])
    qb = _dot(h, wq_ref[...]).astype(jnp.bfloat16)
    kb = _dot(h, wk_ref[...]).astype(jnp.bfloat16)
    q_ref[...] = qb
    k_ref[...] = kb
    vt_ref[...] = _dot_nt(wvt_ref[...], h).astype(jnp.bfloat16)

    def max_sq_norm(zb):
        z = zb.astype(jnp.float32)
        n2 = _dot((z * z).astype(jnp.bfloat16), hsum_ref[...])
        return jnp.max(n2, axis=0, keepdims=True) * NORM_SLACK

    nrm_ref[...] = jnp.concatenate([max_sq_norm(qb), max_sq_norm(kb)], axis=0)


def _inproj(x2, ln, wu, wg, ws, wq, wk, wvt, hsum, batch, seq):
    bt = x2.shape[0]
    tm = TM_IN
    nt = seq // tm
    row = lambda w: pl.BlockSpec((tm, w), lambda i: (i, 0))
    return pl.pallas_call(
        _inproj_kernel,
        grid=(bt // tm,),
        in_specs=[row(D_MODEL), _const_spec((1, D_MODEL)),
                  _const_spec(wu.shape), _const_spec(wg.shape), _const_spec(ws.shape),
                  _const_spec(wq.shape), _const_spec(wk.shape), _const_spec(wvt.shape),
                  _const_spec(hsum.shape)],
        out_specs=[row(POOL_WIDTH), row(4 * GLA_WIDTH), row(LANES), row(FOX_WIDTH), row(FOX_WIDTH),
                   pl.BlockSpec((None, FOX_WIDTH, tm), lambda i: (i // nt, 0, i % nt)),
                   pl.BlockSpec((None, 2, LANES), lambda i: (i, 0, 0))],
        out_shape=[jax.ShapeDtypeStruct((bt, POOL_WIDTH), jnp.float32),
                   jax.ShapeDtypeStruct((bt, 4 * GLA_WIDTH), jnp.bfloat16),
                   jax.ShapeDtypeStruct((bt, LANES), jnp.float32),
                   jax.ShapeDtypeStruct((bt, FOX_WIDTH), jnp.bfloat16),
                   jax.ShapeDtypeStruct((bt, FOX_WIDTH), jnp.bfloat16),
                   jax.ShapeDtypeStruct((batch, FOX_WIDTH, seq), jnp.bfloat16),
                   jax.ShapeDtypeStruct((bt // tm, 2, LANES), jnp.float32)],
        compiler_params=pltpu.CompilerParams(dimension_semantics=("arbitrary",),
                                             vmem_limit_bytes=VMEM_LIMIT),
        name="inproj",
    )(x2, ln, wu, wg, ws, wq, wk, wvt, hsum)


def _mixpg_kernel(u_ref, gla_ref, small_ref,
                  wpool_ref, pscale_ref, wa_hi_ref, wa_lo_ref, ba_ref, gn_ref, bf_ref,
                  tri_chunk_ref, tri_full_ref, pq_ref, cq_ref, pk_ref, ck_ref, hmean_ref,
                  mpg_ref, fq_ref, fk_ref, fedge_ref,
                  ubuf, fcarry, st_ref, *bufs):
    ti = pl.program_id(1)
    assert POOL_WINDOWS == tuple(2 ** (n + 1) for n in range(len(POOL_WINDOWS)))
    assert HALO == 8 * len(POOL_WINDOWS)

    @pl.when(ti == 0)
    def _():
        ubuf[0:HALO, :] = jnp.zeros((HALO, POOL_WIDTH), jnp.float32)
        fcarry[...] = jnp.zeros_like(fcarry)
        st_ref[...] = jnp.zeros_like(st_ref)

    refs = (u_ref, gla_ref, small_ref, wpool_ref, pscale_ref, wa_hi_ref, wa_lo_ref, ba_ref, gn_ref, bf_ref,
            tri_chunk_ref, tri_full_ref, pq_ref, cq_ref, pk_ref, ck_ref, hmean_ref,
            mpg_ref, fq_ref, fk_ref, fedge_ref, ubuf, fcarry, st_ref)
    tiles = [_mix_tile(ti * MIX_SUBTILES + sub, sub, *refs, bufs[sub], bufs[MIX_SUBTILES:])
             for sub in range(MIX_SUBTILES)]
    done = object()
    while tiles:
        tiles = [t for t in tiles if next(t, done) is not done]


def _mix_tile(tile_idx, sub, u_ref, gla_ref, small_ref,
              wpool_ref, pscale_ref, wa_hi_ref, wa_lo_ref, ba_ref, gn_ref, bf_ref,
              tri_chunk_ref, tri_full_ref, pq_ref, cq_ref, pk_ref, ck_ref, hmean_ref,
              mpg_ref, fq_ref, fk_ref, fedge_ref, ubuf, fcarry, st_ref, bcum_ref, pool_bufs):
    tm = TM_MIX
    rows = slice(sub * tm, (sub + 1) * tm)

    u = u_ref[rows, :]
    ubuf[HALO:HALO + tm, :] = u
    total = HALO + tm
    prev = ubuf
    sums = {}
    for n, w in enumerate(POOL_WINDOWS):
        shift, start = w // 2, 8 * (n + 1)
        val = prev[start:total, :] + prev[start - shift:total - shift, :]
        sums[w] = val[HALO - start:, :]
        if n + 1 < len(POOL_WINDOWS):
            pool_bufs[n][start:total, :] = val
            prev = pool_bufs[n]
    lane = lax.broadcasted_iota(jnp.int32, (tm, POOL_WIDTH), 1)
    tpos = tile_idx * tm + lax.broadcasted_iota(jnp.int32, (tm, POOL_WIDTH), 0)
    wsum = sums[POOL_WINDOWS[-1]]
    width = jnp.full((tm, POOL_WIDTH), POOL_WINDOWS[-1], jnp.int32)
    for gi in range(len(POOL_WINDOWS) - 2, -1, -1):
        in_group = lane < (gi + 1) * POOL_GROUP
        wsum = jnp.where(in_group, sums[POOL_WINDOWS[gi]], wsum)
        width = jnp.where(in_group, POOL_WINDOWS[gi], width)
    cnt = jnp.minimum(tpos + 1, width).astype(jnp.float32)
    d = wsum / cnt - u
    y_pool = _dot(d.astype(jnp.bfloat16), wpool_ref[...]) * pscale_ref[...]
    mpg_ref[rows, 0:POOL_WIDTH] = y_pool.astype(jnp.bfloat16)
    ubuf[0:HALO, :] = ubuf[tm:tm + HALO, :]
    yield

    small = small_ref[rows, :]
    s_hi = small.astype(jnp.bfloat16)
    s_lo = (small - s_hi.astype(jnp.float32)).astype(jnp.bfloat16)
    a = (_dot(s_hi, wa_hi_ref[...]) + _dot(s_lo, wa_hi_ref[...]) + _dot(s_hi, wa_lo_ref[...])
         + ba_ref[...])
    la = _log_sigmoid(a) / GLA_TAU
    la_hi, la_mid, la_lo = _split3(la)
    tri_c = tri_chunk_ref[...]
    bcum_ref[...] = _dot(tri_c, la_hi) + _dot(tri_c, la_mid) + _dot(tri_c, la_lo)
    yield

    slane = lax.broadcasted_iota(jnp.int32, (tm, LANES), 1)
    is_ff = (slane >= FF_LANE0) & (slane < FF_LANE0 + FOX_HEADS)
    lf = jnp.where(is_ff, _log_sigmoid(small + bf_ref[...]), 0.0)
    lf_hi, lf_mid, lf_lo = _split3(lf)
    tri_f = tri_full_ref[...]
    fcum = _dot(tri_f, lf_hi) + _dot(tri_f, lf_mid) + _dot(tri_f, lf_lo) + fcarry[...]
    fcarry[...] = fcum[tm - 1:tm, :]
    fl = fcum * LOG2E
    fedge_ref[sub] = jnp.concatenate([fl[0:1, :], fl[tm - 1:tm, :]], axis=0)
    f_hi, f_mid, f_lo = _split3(fl)
    fq_ref[rows, :] = (_dot(f_hi, pq_ref[0]) + _dot(f_mid, pq_ref[1]) + _dot(f_lo, pq_ref[2])
                       + cq_ref[...]).astype(jnp.bfloat16)
    fk_ref[rows, :] = (_dot(f_hi, pk_ref[0]) + _dot(f_mid, pk_ref[1]) + _dot(f_lo, pk_ref[2])
                       + ck_ref[...]).astype(jnp.bfloat16)
    yield

    glane = lax.broadcasted_iota(jnp.int32, (1, GLA_WIDTH), 1)
    head_masks = [(glane >= h * GLA_HEAD_DIM) & (glane < (h + 1) * GLA_HEAD_DIM)
                  for h in range(GLA_HEADS)]
    r_i = lax.broadcasted_iota(jnp.int32, (GLA_WIDTH, GLA_WIDTH), 0) // GLA_HEAD_DIM
    c_i = lax.broadcasted_iota(jnp.int32, (GLA_WIDTH, GLA_WIDTH), 1) // GLA_HEAD_DIM
    same_head = r_i == c_i
    n_chunks = tm // CHUNK

    def chunk_row(offset):
        return jnp.concatenate(
            [jnp.broadcast_to(bcum_ref[c * CHUNK + offset:c * CHUNK + offset + 1, :], (CHUNK, GLA_WIDTH))
             for c in range(n_chunks)], axis=0)

    q = gla_ref[rows, 0:GLA_WIDTH].astype(jnp.float32)
    k = gla_ref[rows, GLA_WIDTH:2 * GLA_WIDTH].astype(jnp.float32)
    v = gla_ref[rows, 2 * GLA_WIDTH:3 * GLA_WIDTH]
    bc = bcum_ref[...]
    b_mid = chunk_row(CHUNK // 2 - 1)
    b_last = chunk_row(CHUNK - 1)
    q_in = (q * jnp.exp(bc - b_mid)).astype(jnp.bfloat16)
    k_in = (k * jnp.exp(b_mid - bc)).astype(jnp.bfloat16)
    k_kv = (k * jnp.exp(b_last - bc)).astype(jnp.bfloat16)
    q_st = (q * jnp.exp(bc)).astype(jnp.bfloat16)
    yield

    in_chunk_causal = tri_c > 0
    att = jnp.concatenate(
        [jnp.where(in_chunk_causal,
                   _dot_nt(jnp.where(head_masks[h], q_in, jnp.zeros_like(q_in)), k_in),
                   0.0).astype(jnp.bfloat16) for h in range(GLA_HEADS)], axis=1)
    v_heads = jnp.concatenate([jnp.where(head_masks[h], v, jnp.zeros_like(v))
                               for h in range(GLA_HEADS)], axis=0)
    o_intra = _dot(att, v_heads)
    chunk_rows = [slice(c * CHUNK, (c + 1) * CHUNK) for c in range(n_chunks)]
    kv_t = [jnp.where(same_head, _dot_tn(v[cr], k_kv[cr]), 0.0) for cr in chunk_rows]
    yield

    st = st_ref[...]
    o_inter = []
    for c, cr in enumerate(chunk_rows):
        o_inter.append(_dot_nt(q_st[cr], st.astype(jnp.bfloat16)))
        dec = jnp.exp(bcum_ref[(c + 1) * CHUNK - 1:(c + 1) * CHUNK, :])
        st = st * dec + kv_t[c]
    st_ref[...] = st
    yield

    o = o_intra + jnp.concatenate(o_inter, axis=0)
    o2 = o * o
    o2_hi = o2.astype(jnp.bfloat16)
    o2_lo = (o2 - o2_hi.astype(jnp.float32)).astype(jnp.bfloat16)
    mean_sq = _dot(o2_hi, hmean_ref[...]) + _dot(o2_lo, hmean_ref[...])
    g = gla_ref[rows, 3 * GLA_WIDTH:4 * GLA_WIDTH].astype(jnp.float32)
    y_gla = o * lax.rsqrt(mean_sq + EPS) * gn_ref[...] * (g * jax.nn.sigmoid(g))
    mpg_ref[rows, POOL_WIDTH:POOL_WIDTH + GLA_WIDTH] = y_gla.astype(jnp.bfloat16)


def _mixpg(u, gla, small, consts, batch, seq):
    tm = TM_MIX
    step = MIX_SUBTILES * tm
    nt = seq // step
    row = lambda w: pl.BlockSpec((step, w), lambda b, i: (b * nt + i, 0))
    bt = u.shape[0]
    return pl.pallas_call(
        _mixpg_kernel,
        grid=(batch, nt),
        in_specs=[row(POOL_WIDTH), row(4 * GLA_WIDTH), row(LANES)] + [_const_spec(c.shape) for c in consts],
        out_specs=[row(POOL_WIDTH + GLA_WIDTH), row(FOX_WIDTH), row(FOX_WIDTH),
                   pl.BlockSpec((MIX_SUBTILES, 2, LANES), lambda b, i: (b * nt + i, 0, 0))],
        out_shape=[jax.ShapeDtypeStruct((bt, POOL_WIDTH + GLA_WIDTH), jnp.bfloat16),
                   jax.ShapeDtypeStruct((bt, FOX_WIDTH), jnp.bfloat16),
                   jax.ShapeDtypeStruct((bt, FOX_WIDTH), jnp.bfloat16),
                   jax.ShapeDtypeStruct((bt // tm, 2, LANES), jnp.float32)],
        scratch_shapes=[pltpu.VMEM((tm + HALO, POOL_WIDTH), jnp.float32),
                        pltpu.VMEM((1, LANES), jnp.float32),
                        pltpu.VMEM((GLA_WIDTH, GLA_WIDTH), jnp.float32)]
                       + [pltpu.VMEM((tm, GLA_WIDTH), jnp.float32)] * MIX_SUBTILES
                       + [pltpu.VMEM((tm + HALO, POOL_WIDTH), jnp.float32)] * (len(POOL_WINDOWS) - 1),
        compiler_params=pltpu.CompilerParams(dimension_semantics=("arbitrary", "arbitrary"),
                                             vmem_limit_bytes=VMEM_LIMIT),
        name="mixpg",
    )(u, gla, small, *consts)


def _fox_kernel(first_ref, q_ref, fq_ref, k_ref, fk_ref, vt_ref, o_ref, *scratch):
    nh = FOX_HEADS_PER_STEP
    qh, s_, p_, bm, m_, al, acc = (scratch[i * nh:(i + 1) * nh] for i in range(7))
    qi = pl.program_id(2)
    j0 = first_ref[(pl.program_id(0) * pl.num_programs(1) + pl.program_id(1)) * pl.num_programs(2) + qi]

    def load_queries(q_tile, fq_tile):
        row = lax.broadcasted_iota(jnp.int32, (2 * LANES, 1), 0)
        for pair in range(nh // 2):
            cols = slice(pair * LANES, (pair + 1) * LANES)
            qext_t = jnp.concatenate([q_tile[:, cols], fq_tile[:, cols]], axis=1).astype(jnp.float32).T
            for sub in range(2):
                lo = sub * FOX_HEAD_DIM
                flo = LANES + sub * F_SLOT
                own = ((row >= lo) & (row < lo + FOX_HEAD_DIM)) | ((row >= flo) & (row < flo + F_SLOT))
                qh[2 * pair + sub][...] = jnp.where(own, qext_t, 0.0).astype(jnp.bfloat16)

    def scores(j, h):
        k0 = pl.multiple_of(j * TQ, TQ)
        cols = slice((h // 2) * LANES, (h // 2 + 1) * LANES)
        kext = jnp.concatenate([k_ref[pl.ds(k0, TQ), cols], fk_ref[pl.ds(k0, TQ), cols]], axis=1)
        s = _dot(kext, qh[h][...])
        s_[h][...] = s
        bm[h][...] = jnp.max(s, axis=0, keepdims=True)

    def softmax(h, diagonal):
        s = s_[h][...]
        if diagonal:
            krow = lax.broadcasted_iota(jnp.int32, (TQ, TQ), 0)
            qcol = lax.broadcasted_iota(jnp.int32, (TQ, TQ), 1)
            s = jnp.where(krow <= qcol, s, NEG_BIG)
            blk_max = jnp.max(s, axis=0, keepdims=True)
        else:
            blk_max = bm[h][...]
        m_old = m_[h][...]
        m_new = jnp.maximum(m_old, blk_max)
        al[h][...] = jnp.exp2(m_old - m_new)
        p_[h][...] = jnp.exp2((s - m_new).astype(jnp.bfloat16))
        m_[h][...] = m_new

    def update(j, h):
        k0 = pl.multiple_of(j * TQ, TQ)
        vth = jnp.concatenate([vt_ref[h * FOX_HEAD_DIM:(h + 1) * FOX_HEAD_DIM, pl.ds(k0, TQ)],
                               jnp.ones((16, TQ), jnp.bfloat16)], axis=0)
        acc[h][...] = al[h][...] * acc[h][...] + _dot(vth, p_[h][...])

    @pl.when((pl.program_id(0) == 0) & (pl.program_id(1) == 0) & (qi == 0))
    def _():
        for h in range(nh):
            p_[h][...] = jnp.zeros_like(p_[h])

    load_queries(q_ref, fq_ref)
    for h in range(nh):
        scores(j0, h)

    for h in range(nh):
        acc[h][...] = jnp.zeros_like(acc[h])
        m_[h][...] = jnp.full(m_[h].shape, NEG_BIG, jnp.float32)
        al[h][...] = jnp.ones_like(al[h])

    def body(j, carry):
        prev = jnp.maximum(j - 1, j0)
        for h in range(nh):
            update(prev, h)
            softmax(h, False)
            scores(j + 1, h)
        return carry

    lax.fori_loop(j0, qi, body, 0)
    prev = jnp.maximum(qi - 1, j0)
    for h in range(nh):
        update(prev, h)
        softmax(h, True)
    for h in range(nh):
        update(qi, h)
    for pair in range(nh // 2):
        o_t = jnp.concatenate([a[0:FOX_HEAD_DIM, :] / a[FOX_HEAD_DIM:FOX_HEAD_DIM + 1, :]
                               for a in (acc[2 * pair][...], acc[2 * pair + 1][...])], axis=0)
        o_ref[:, pair * LANES:(pair + 1) * LANES] = o_t.T.astype(jnp.bfloat16)


def _first_blocks(nrm, fedge, batch, seq):
    nq = seq // TQ
    groups = FOX_HEADS // FOX_HEADS_PER_STEP
    heads = slice(0, FOX_HEADS)
    norms = jnp.sqrt(nrm.reshape(batch, nq, 2, LANES)[..., heads])
    qn, kn = norms[:, :, 0], norms[:, :, 1]
    fe = fedge.reshape(batch, nq, TQ // TM_MIX, 2, LANES)[..., FF_LANE0:FF_LANE0 + FOX_HEADS]
    f_start, f_end = fe[:, :, 0, 0], fe[:, :, -1, 1]
    bound = (qn[:, :, None] * (kn[:, None, :] + kn[:, :, None])
             + f_start[:, :, None] - f_end[:, None, :])
    j_idx = lax.broadcasted_iota(jnp.int32, bound.shape, 2)
    i_idx = lax.broadcasted_iota(jnp.int32, bound.shape, 1)
    skippable = (bound <= -PRUNE_LOG2) & (j_idx != i_idx)
    first_h = jnp.min(jnp.where(skippable, nq, j_idx), axis=2)
    first = jnp.min(first_h.reshape(batch, nq, groups, FOX_HEADS_PER_STEP), axis=-1)
    return first.transpose(0, 2, 1).reshape(-1)


def _fox(first, q, fq, k, fk, vt, batch, seq):
    nq = seq // TQ
    nh = FOX_HEADS_PER_STEP
    width = nh * FOX_HEAD_DIM
    groups = FOX_HEADS // nh
    qspec = pl.BlockSpec((TQ, width), lambda b, g, i, first: (b * nq + i, g))
    kspec = pl.BlockSpec((seq, width), lambda b, g, i, first: (b, g))
    vspec = pl.BlockSpec((None, width, seq), lambda b, g, i, first: (b, g, 0))
    acc_rows = FOX_HEAD_DIM + 16
    return pl.pallas_call(
        _fox_kernel,
        grid_spec=pltpu.PrefetchScalarGridSpec(
            num_scalar_prefetch=1,
            grid=(batch, groups, nq),
            in_specs=[qspec, qspec, kspec, kspec, vspec],
            out_specs=qspec,
            scratch_shapes=([pltpu.VMEM((2 * LANES, TQ), jnp.bfloat16)] * nh
                            + [pltpu.VMEM((TQ, TQ), jnp.float32)] * nh
                            + [pltpu.VMEM((TQ, TQ), jnp.bfloat16)] * nh
                            + [pltpu.VMEM((1, TQ), jnp.float32)] * (3 * nh)
                            + [pltpu.VMEM((acc_rows, TQ), jnp.float32)] * nh)),
        out_shape=jax.ShapeDtypeStruct(q.shape, jnp.bfloat16),
        compiler_params=pltpu.CompilerParams(
            dimension_semantics=("arbitrary", "arbitrary", "arbitrary"),
            vmem_limit_bytes=VMEM_LIMIT),
        name="fox",
    )(first, q, fq, k, fk, vt)


def _outffn_kernel(x_ref, mpg_ref, yf_ref, wo_ref, ln2_ref, wgu_ref, wdn_ref, lnf_ref, o_ref, *, final):
    half = POOL_WIDTH + GLA_WIDTH
    x1 = (x_ref[...] + _dot(mpg_ref[...], wo_ref[0:half, :]) + _dot(yf_ref[...], wo_ref[half:, :]))
    ms = jnp.mean(x1 * x1, axis=-1, keepdims=True)
    h = (x1 * lax.rsqrt(ms + EPS) * ln2_ref[...]).astype(jnp.bfloat16)
    gate = _dot(h, wgu_ref[:, 0:D_FF])
    up = _dot(h, wgu_ref[:, D_FF:])
    act = (gate * jax.nn.sigmoid(gate) * up).astype(jnp.bfloat16)
    x2 = x1 + _dot(act, wdn_ref[...])
    if final:
        ms2 = jnp.mean(x2 * x2, axis=-1, keepdims=True)
        x2 = x2 * lax.rsqrt(ms2 + EPS) * lnf_ref[...]
    o_ref[...] = x2


def _outffn(x2, mpg, yf, wo, ln2, wgu, wdn, lnf, final):
    bt = x2.shape[0]
    tm = TM_FFN
    row = lambda w: pl.BlockSpec((tm, w), lambda i: (i, 0))
    single = lambda shape: pl.BlockSpec(shape, lambda i: (0, 0), pipeline_mode=pl.Buffered(1))
    return pl.pallas_call(
        functools.partial(_outffn_kernel, final=final),
        grid=(bt // tm,),
        in_specs=[row(D_MODEL), row(POOL_WIDTH + GLA_WIDTH), row(FOX_WIDTH),
                  single(wo.shape), _const_spec((1, D_MODEL)), single(wgu.shape), single(wdn.shape),
                  _const_spec((1, D_MODEL))],
        out_specs=row(D_MODEL),
        out_shape=jax.ShapeDtypeStruct((bt, D_MODEL), jnp.float32),
        compiler_params=pltpu.CompilerParams(dimension_semantics=("arbitrary",),
                                             vmem_limit_bytes=VMEM_LIMIT),
        name="outffn",
    )(x2, mpg, yf, wo, ln2, wgu, wdn, lnf)


def _placement(dtype=jnp.bfloat16):
    pq = np.zeros((3, LANES, FOX_WIDTH), np.float32)
    pk = np.zeros((3, LANES, FOX_WIDTH), np.float32)
    cq = np.zeros((1, FOX_WIDTH), np.float32)
    ck = np.zeros((1, FOX_WIDTH), np.float32)
    for h in range(FOX_HEADS):
        base = (h // 2) * LANES + (h % 2) * F_SLOT
        for part in range(3):
            pq[part, FF_LANE0 + h, base + part] = 1.0
            pk[part, FF_LANE0 + h, base + 3 + part] = -1.0
            cq[0, base + 3 + part] = 1.0
            ck[0, base + part] = 1.0
    return jnp.asarray(pq, dtype), jnp.asarray(cq), jnp.asarray(pk, dtype), jnp.asarray(ck)


def _triangles(dtype=jnp.bfloat16):
    r = np.arange(TM_MIX)[:, None]
    c = np.arange(TM_MIX)[None, :]
    full = (r >= c)
    chunk = full & (r // CHUNK == c // CHUNK)
    return jnp.asarray(chunk, dtype), jnp.asarray(full, dtype)


@jax.jit
def kernel(x, ln1, w_in, w_pool, pool_scale, w_a_up, b_a, gla_gn, b_f, w_o, ln2, w_gu, w_down, ln_f):
    batch, seq, _ = x.shape
    depth = w_in.shape[0]
    bf16 = jnp.bfloat16
    xf = x.reshape(batch * seq, D_MODEL)

    tri_chunk, tri_full = _triangles()
    pq, cq, pk, ck = _placement()
    group_of = np.arange(POOL_WIDTH) // POOL_GROUP
    same_group = jnp.asarray(group_of[:, None] == group_of[None, :])
    head_sum = jnp.asarray(np.arange(FOX_WIDTH)[:, None] // FOX_HEAD_DIM == np.arange(LANES)[None, :], bf16)
    assert TM_IN == TQ and TQ % TM_MIX == 0
    gla_head = np.arange(GLA_WIDTH) // GLA_HEAD_DIM
    head_mean = jnp.asarray((gla_head[:, None] == gla_head[None, :]) / GLA_HEAD_DIM, bf16)

    o0 = POOL_WIDTH
    o1 = o0 + 4 * GLA_WIDTH
    o2 = o1 + GLA_GATE_RANK
    o3 = o2 + 3 * FOX_WIDTH
    col_scale = np.ones((o3 + FOX_HEADS,), np.float32)
    col_scale[o0:o0 + GLA_WIDTH] = GLA_HEAD_DIM ** -0.5
    col_scale[o2:o2 + FOX_WIDTH] = FOX_HEAD_DIM ** -0.5 * LOG2E
    for l in range(depth):
        w = (w_in[l] * col_scale).astype(bf16)
        wu = w[:, 0:o0]
        wg = w[:, o0:o1]
        ws = jnp.pad(jnp.concatenate([w[:, o1:o2], w[:, o3:o3 + FOX_HEADS]], axis=1),
                     ((0, 0), (0, LANES - GLA_GATE_RANK - FOX_HEADS)))
        wq = w[:, o2:o2 + FOX_WIDTH]
        wk = w[:, o2 + FOX_WIDTH:o2 + 2 * FOX_WIDTH]
        wvt = w[:, o2 + 2 * FOX_WIDTH:o3].T
        u, gla, small, q, k, vt, nrm = _inproj(xf, ln1[l][None, :], wu, wg, ws, wq, wk, wvt, head_sum,
                                               batch, seq)

        wp = w_pool[l].reshape(POOL_WIDTH, POOL_GROUP)
        wpool_bd = jnp.where(same_group, jnp.tile(wp, (1, len(POOL_WINDOWS))), 0.0).astype(bf16)
        wa = jnp.pad(w_a_up[l], ((0, LANES - GLA_GATE_RANK), (0, 0)))
        wa_hi = wa.astype(bf16)
        wa_lo = (wa - wa_hi.astype(jnp.float32)).astype(bf16)
        bf_row = jnp.pad(b_f[l][None, :], ((0, 0), (FF_LANE0, LANES - FF_LANE0 - FOX_HEADS)))
        consts = [wpool_bd, pool_scale[l][None, :], wa_hi, wa_lo, b_a[l][None, :],
                  gla_gn[l][None, :], bf_row, tri_chunk, tri_full, pq, cq, pk, ck, head_mean]
        mpg, fq, fk, fedge = _mixpg(u, gla, small, consts, batch, seq)

        yf = _fox(_first_blocks(nrm, fedge, batch, seq), q, fq, k, fk, vt, batch, seq)

        xf = _outffn(xf, mpg, yf, w_o[l].astype(bf16), ln2[l][None, :], w_gu[l].astype(bf16),
                     w_down[l].astype(bf16), ln_f[None, :], final=(l == depth - 1))
    return xf.reshape(batch, seq, D_MODEL)
```

```python
import functools

import jax
import jax.numpy as jnp
import numpy as np
from jax import lax
from jax.experimental import pallas as pl
from jax.experimental.pallas import tpu as pltpu

D_MODEL = 1024
CHUNK = 64
EPS = 1e-6
POOL_WIDTH = 256
POOL_WINDOWS = (2, 4, 8, 16)
POOL_GROUP = 64
GLA_WIDTH = 256
GLA_HEADS = 4
GLA_HEAD_DIM = 64
GLA_GATE_RANK = 16
GLA_TAU = 16.0
FOX_WIDTH = 512
FOX_HEAD_DIM = 64
FOX_HEADS = 8
D_FF = 2816

LANES = 128
HALO = 32
FF_LANE0 = GLA_GATE_RANK
F_SLOT = 8
NEG_BIG = -1e30
LOG2E = 1.4426950408889634
PRUNE_LOG2 = 150.0
NORM_SLACK = 1.01

VMEM_LIMIT = 56 * 1024 * 1024

TM_IN = 512
TM_MIX = 256
MIX_SUBTILES = 8
TQ = 512
FOX_HEADS_PER_STEP = 4
TM_FFN = 512


def _dot(a, b):
    return jnp.dot(a, b, preferred_element_type=jnp.float32)


def _dot_nt(a, b):
    return lax.dot_general(a, b, (((1,), (1,)), ((), ())), preferred_element_type=jnp.float32)


def _dot_tn(a, b):
    return lax.dot_general(a, b, (((0,), (0,)), ((), ())), preferred_element_type=jnp.float32)


def _split3(x):
    hi = x.astype(jnp.bfloat16)
    r = x - hi.astype(jnp.float32)
    mid = r.astype(jnp.bfloat16)
    lo = (r - mid.astype(jnp.float32)).astype(jnp.bfloat16)
    return hi, mid, lo


def _log_sigmoid(x):
    return jnp.minimum(x, 0.0) - jnp.log(1.0 + jnp.exp(-jnp.abs(x)))


def _const_spec(shape):
    nd = len(shape)
    return pl.BlockSpec(shape, lambda *_: (0,) * nd)


def _inproj_kernel(x_ref, ln_ref, wu_ref, wg_ref, ws_ref, wq_ref, wk_ref, wvt_ref, hsum_ref,
                   u_ref, gla_ref, small_ref, q_ref, k_ref, vt_ref, nrm_ref):
    x = x_ref[...]
    ms = jnp.mean(x * x, axis=-1, keepdims=True)
    h = (x * lax.rsqrt(ms + EPS) * ln_ref[...]).astype(jnp.bfloat16)
    u_ref[...] = _dot(h, wu_ref[...])
    gla_ref[...] = _dot(h, wg_ref[...]).astype(jnp.bfloat16)
    small_ref[...] = _dot(h, ws_ref[...])
    qb = _dot(h, wq_ref[...]).astype(jnp.bfloat16)
    kb = _dot(h, wk_ref[...]).astype(jnp.bfloat16)
    q_ref[...] = qb
    k_ref[...] = kb
    vt_ref[...] = _dot_nt(wvt_ref[...], h).astype(jnp.bfloat16)

    def max_sq_norm(zb):
        z = zb.astype(jnp.float32)
        n2 = _dot((z * z).astype(jnp.bfloat16), hsum_ref[...])
        return jnp.max(n2, axis=0, keepdims=True) * NORM_SLACK

    nrm_ref[...] = jnp.concatenate([max_sq_norm(qb), max_sq_norm(kb)], axis=0)


def _inproj(x2, ln, wu, wg, ws, wq, wk, wvt, hsum, batch, seq):
    bt = x2.shape[0]
    tm = TM_IN
    nt = seq // tm
    row = lambda w: pl.BlockSpec((tm, w), lambda i: (i, 0))
    return pl.pallas_call(
        _inproj_kernel,
        grid=(bt // tm,),
        in_specs=[row(D_MODEL), _const_spec((1, D_MODEL)),
                  _const_spec(wu.shape), _const_spec(wg.shape), _const_spec(ws.shape),
                  _const_spec(wq.shape), _const_spec(wk.shape), _const_spec(wvt.shape),
                  _const_spec(hsum.shape)],
        out_specs=[row(POOL_WIDTH), row(4 * GLA_WIDTH), row(LANES), row(FOX_WIDTH), row(FOX_WIDTH),
                   pl.BlockSpec((None, FOX_WIDTH, tm), lambda i: (i // nt, 0, i % nt)),
                   pl.BlockSpec((None, 2, LANES), lambda i: (i, 0, 0))],
        out_shape=[jax.ShapeDtypeStruct((bt, POOL_WIDTH), jnp.float32),
                   jax.ShapeDtypeStruct((bt, 4 * GLA_WIDTH), jnp.bfloat16),
                   jax.ShapeDtypeStruct((bt, LANES), jnp.float32),
                   jax.ShapeDtypeStruct((bt, FOX_WIDTH), jnp.bfloat16),
                   jax.ShapeDtypeStruct((bt, FOX_WIDTH), jnp.bfloat16),
                   jax.ShapeDtypeStruct((batch, FOX_WIDTH, seq), jnp.bfloat16),
                   jax.ShapeDtypeStruct((bt // tm, 2, LANES), jnp.float32)],
        compiler_params=pltpu.CompilerParams(dimension_semantics=("arbitrary",),
                                             vmem_limit_bytes=VMEM_LIMIT),
        name="inproj",
    )(x2, ln, wu, wg, ws, wq, wk, wvt, hsum)


def _mixpg_kernel(u_ref, gla_ref, small_ref,
                  wpool_ref, pscale_ref, wa_hi_ref, wa_lo_ref, ba_ref, gn_ref, bf_ref,
                  tri_chunk_ref, tri_full_ref, place_ref, const_ref, hmean_ref,
                  mpg_ref, fqk_ref, fedge_ref,
                  ubuf, fcarry, st_ref, *bufs):
    ti = pl.program_id(1)
    assert POOL_WINDOWS == tuple(2 ** (n + 1) for n in range(len(POOL_WINDOWS)))
    assert HALO == 8 * len(POOL_WINDOWS)

    @pl.when(ti == 0)
    def _():
        ubuf[0:HALO, :] = jnp.zeros((HALO, POOL_WIDTH), jnp.float32)
        fcarry[...] = jnp.zeros_like(fcarry)
        st_ref[...] = jnp.zeros_like(st_ref)

    refs = (u_ref, gla_ref, small_ref, wpool_ref, pscale_ref, wa_hi_ref, wa_lo_ref, ba_ref, gn_ref, bf_ref,
            tri_chunk_ref, tri_full_ref, place_ref, const_ref, hmean_ref,
            mpg_ref, fqk_ref, fedge_ref, ubuf, fcarry, st_ref)
    tiles = [_mix_tile(ti * MIX_SUBTILES + sub, sub, *refs, bufs[sub], bufs[MIX_SUBTILES:])
             for sub in range(MIX_SUBTILES)]
    done = object()
    while tiles:
        tiles = [t for t in tiles if next(t, done) is not done]


def _mix_tile(tile_idx, sub, u_ref, gla_ref, small_ref,
              wpool_ref, pscale_ref, wa_hi_ref, wa_lo_ref, ba_ref, gn_ref, bf_ref,
              tri_chunk_ref, tri_full_ref, place_ref, const_ref, hmean_ref,
              mpg_ref, fqk_ref, fedge_ref, ubuf, fcarry, st_ref, bcum_ref, pool_bufs):
    tm = TM_MIX
    rows = slice(sub * tm, (sub + 1) * tm)

    u = u_ref[rows, :]
    ubuf[HALO:HALO + tm, :] = u
    total = HALO + tm
    prev = ubuf
    sums = {}
    for n, w in enumerate(POOL_WINDOWS):
        shift, start = w // 2, 8 * (n + 1)
        val = prev[start:total, :] + prev[start - shift:total - shift, :]
        sums[w] = val[HALO - start:, :]
        if n + 1 < len(POOL_WINDOWS):
            pool_bufs[n][start:total, :] = val
            prev = pool_bufs[n]
    lane = lax.broadcasted_iota(jnp.int32, (tm, POOL_WIDTH), 1)
    tpos = tile_idx * tm + lax.broadcasted_iota(jnp.int32, (tm, POOL_WIDTH), 0)
    wsum = sums[POOL_WINDOWS[-1]]
    width = jnp.full((tm, POOL_WIDTH), POOL_WINDOWS[-1], jnp.int32)
    for gi in range(len(POOL_WINDOWS) - 2, -1, -1):
        in_group = lane < (gi + 1) * POOL_GROUP
        wsum = jnp.where(in_group, sums[POOL_WINDOWS[gi]], wsum)
        width = jnp.where(in_group, POOL_WINDOWS[gi], width)
    cnt = jnp.minimum(tpos + 1, width).astype(jnp.float32)
    d = wsum / cnt - u
    y_pool = _dot(d.astype(jnp.bfloat16), wpool_ref[...]) * pscale_ref[...]
    mpg_ref[rows, 0:POOL_WIDTH] = y_pool.astype(jnp.bfloat16)
    ubuf[0:HALO, :] = ubuf[tm:tm + HALO, :]
    yield

    small = small_ref[rows, :]
    s_hi = small.astype(jnp.bfloat16)
    s_lo = (small - s_hi.astype(jnp.float32)).astype(jnp.bfloat16)
    a = (_dot(s_hi, wa_hi_ref[...]) + _dot(s_lo, wa_hi_ref[...]) + _dot(s_hi, wa_lo_ref[...])
         + ba_ref[...])
    la = _log_sigmoid(a) / GLA_TAU
    la_hi, la_mid, la_lo = _split3(la)
    tri_c = tri_chunk_ref[...]
    bcum_ref[...] = _dot(tri_c, la_hi) + _dot(tri_c, la_mid) + _dot(tri_c, la_lo)
    yield

    slane = lax.broadcasted_iota(jnp.int32, (tm, LANES), 1)
    is_ff = (slane >= FF_LANE0) & (slane < FF_LANE0 + FOX_HEADS)
    lf = jnp.where(is_ff, _log_sigmoid(small + bf_ref[...]), 0.0)
    lf_hi, lf_mid, lf_lo = _split3(lf)
    tri_f = tri_full_ref[...]
    fcum = _dot(tri_f, lf_hi) + _dot(tri_f, lf_mid) + _dot(tri_f, lf_lo) + fcarry[...]
    fcarry[...] = fcum[tm - 1:tm, :]
    fl = fcum * LOG2E
    fedge_ref[sub] = jnp.concatenate([fl[0:1, :], fl[tm - 1:tm, :]], axis=0)
    f_hi, f_mid, f_lo = _split3(fl)
    fqk_ref[rows, :] = (_dot(f_hi, place_ref[0]) + _dot(f_mid, place_ref[1]) + _dot(f_lo, place_ref[2])
                        + const_ref[...]).astype(jnp.bfloat16)
    yield

    glane = lax.broadcasted_iota(jnp.int32, (1, GLA_WIDTH), 1)
    head_masks = [(glane >= h * GLA_HEAD_DIM) & (glane < (h + 1) * GLA_HEAD_DIM)
                  for h in range(GLA_HEADS)]
    r_i = lax.broadcasted_iota(jnp.int32, (GLA_WIDTH, GLA_WIDTH), 0) // GLA_HEAD_DIM
    c_i = lax.broadcasted_iota(jnp.int32, (GLA_WIDTH, GLA_WIDTH), 1) // GLA_HEAD_DIM
    same_head = r_i == c_i
    n_chunks = tm // CHUNK

    def chunk_row(offset):
        return jnp.concatenate(
            [jnp.broadcast_to(bcum_ref[c * CHUNK + offset:c * CHUNK + offset + 1, :], (CHUNK, GLA_WIDTH))
             for c in range(n_chunks)], axis=0)

    q = gla_ref[rows, 0:GLA_WIDTH].astype(jnp.float32)
    k = gla_ref[rows, GLA_WIDTH:2 * GLA_WIDTH].astype(jnp.float32)
    v = gla_ref[rows, 2 * GLA_WIDTH:3 * GLA_WIDTH]
    bc = bcum_ref[...]
    b_mid = chunk_row(CHUNK // 2 - 1)
    b_last = chunk_row(CHUNK - 1)
    q_in = (q * jnp.exp(bc - b_mid)).astype(jnp.bfloat16)
    k_in = (k * jnp.exp(b_mid - bc)).astype(jnp.bfloat16)
    k_kv = (k * jnp.exp(b_last - bc)).astype(jnp.bfloat16)
    q_st = (q * jnp.exp(bc)).astype(jnp.bfloat16)
    yield

    in_chunk_causal = tri_c > 0
    att = jnp.concatenate(
        [jnp.where(in_chunk_causal,
                   _dot_nt(jnp.where(head_masks[h], q_in, jnp.zeros_like(q_in)), k_in),
                   0.0).astype(jnp.bfloat16) for h in range(GLA_HEADS)], axis=1)
    v_heads = jnp.concatenate([jnp.where(head_masks[h], v, jnp.zeros_like(v))
                               for h in range(GLA_HEADS)], axis=0)
    o_intra = _dot(att, v_heads)
    chunk_rows = [slice(c * CHUNK, (c + 1) * CHUNK) for c in range(n_chunks)]
    kv_t = [jnp.where(same_head, _dot_tn(v[cr], k_kv[cr]), 0.0) for cr in chunk_rows]
    yield

    st = st_ref[...]
    o_inter = []
    for c, cr in enumerate(chunk_rows):
        o_inter.append(_dot_nt(q_st[cr], st.astype(jnp.bfloat16)))
        dec = jnp.exp(bcum_ref[(c + 1) * CHUNK - 1:(c + 1) * CHUNK, :])
        st = st * dec + kv_t[c]
    st_ref[...] = st
    yield

    o = o_intra + jnp.concatenate(o_inter, axis=0)
    o2 = o * o
    o2_hi = o2.astype(jnp.bfloat16)
    o2_lo = (o2 - o2_hi.astype(jnp.float32)).astype(jnp.bfloat16)
    mean_sq = _dot(o2_hi, hmean_ref[...]) + _dot(o2_lo, hmean_ref[...])
    g = gla_ref[rows, 3 * GLA_WIDTH:4 * GLA_WIDTH].astype(jnp.float32)
    y_gla = o * lax.rsqrt(mean_sq + EPS) * gn_ref[...] * (g * jax.nn.sigmoid(g))
    mpg_ref[rows, POOL_WIDTH:POOL_WIDTH + GLA_WIDTH] = y_gla.astype(jnp.bfloat16)


def _mixpg(u, gla, small, consts, batch, seq):
    tm = TM_MIX
    step = MIX_SUBTILES * tm
    nt = seq // step
    row = lambda w: pl.BlockSpec((step, w), lambda b, i: (b * nt + i, 0))
    bt = u.shape[0]
    return pl.pallas_call(
        _mixpg_kernel,
        grid=(batch, nt),
        in_specs=[row(POOL_WIDTH), row(4 * GLA_WIDTH), row(LANES)] + [_const_spec(c.shape) for c in consts],
        out_specs=[row(POOL_WIDTH + GLA_WIDTH), row(2 * LANES),
                   pl.BlockSpec((MIX_SUBTILES, 2, LANES), lambda b, i: (b * nt + i, 0, 0))],
        out_shape=[jax.ShapeDtypeStruct((bt, POOL_WIDTH + GLA_WIDTH), jnp.bfloat16),
                   jax.ShapeDtypeStruct((bt, 2 * LANES), jnp.bfloat16),
                   jax.ShapeDtypeStruct((bt // tm, 2, LANES), jnp.float32)],
        scratch_shapes=[pltpu.VMEM((tm + HALO, POOL_WIDTH), jnp.float32),
                        pltpu.VMEM((1, LANES), jnp.float32),
                        pltpu.VMEM((GLA_WIDTH, GLA_WIDTH), jnp.float32)]
                       + [pltpu.VMEM((tm, GLA_WIDTH), jnp.float32)] * MIX_SUBTILES
                       + [pltpu.VMEM((tm + HALO, POOL_WIDTH), jnp.float32)] * (len(POOL_WINDOWS) - 1),
        compiler_params=pltpu.CompilerParams(dimension_semantics=("arbitrary", "arbitrary"),
                                             vmem_limit_bytes=VMEM_LIMIT),
        name="mixpg",
    )(u, gla, small, *consts)


def _fox_kernel(first_ref, q_ref, fq_ref, k_ref, fk_ref, vt_ref, o_ref, *scratch):
    nh = FOX_HEADS_PER_STEP
    qh, s_, p_, bm, m_, al, acc = (scratch[i * nh:(i + 1) * nh] for i in range(7))
    qi = pl.program_id(2)
    j0 = first_ref[(pl.program_id(0) * pl.num_programs(1) + pl.program_id(1)) * pl.num_programs(2) + qi]

    def load_queries(q_tile, fq_tile):
        row = lax.broadcasted_iota(jnp.int32, (2 * LANES, 1), 0)
        first_head = pl.program_id(1) * nh
        for pair in range(nh // 2):
            cols = slice(pair * LANES, (pair + 1) * LANES)
            qext_t = jnp.concatenate([q_tile[:, cols], fq_tile[...]], axis=1).astype(jnp.float32).T
            for sub in range(2):
                lo = sub * FOX_HEAD_DIM
                flo = LANES + (first_head + 2 * pair + sub) * F_SLOT
                own = ((row >= lo) & (row < lo + FOX_HEAD_DIM)) | ((row >= flo) & (row < flo + F_SLOT))
                qh[2 * pair + sub][...] = jnp.where(own, qext_t, 0.0).astype(jnp.bfloat16)

    def scores(j, h):
        k0 = pl.multiple_of(j * TQ, TQ)
        cols = slice((h // 2) * LANES, (h // 2 + 1) * LANES)
        kext = jnp.concatenate([k_ref[pl.ds(k0, TQ), cols], fk_ref[pl.ds(k0, TQ), :]], axis=1)
        s = _dot(kext, qh[h][...])
        s_[h][...] = s
        bm[h][...] = jnp.max(s, axis=0, keepdims=True)

    def softmax(h, diagonal):
        s = s_[h][...]
        if diagonal:
            krow = lax.broadcasted_iota(jnp.int32, (TQ, TQ), 0)
            qcol = lax.broadcasted_iota(jnp.int32, (TQ, TQ), 1)
            s = jnp.where(krow <= qcol, s, NEG_BIG)
            blk_max = jnp.max(s, axis=0, keepdims=True)
        else:
            blk_max = bm[h][...]
        m_old = m_[h][...]
        m_new = jnp.maximum(m_old, blk_max)
        al[h][...] = jnp.exp2(m_old - m_new)
        p_[h][...] = jnp.exp2((s - m_new).astype(jnp.bfloat16))
        m_[h][...] = m_new

    def update(j, h):
        k0 = pl.multiple_of(j * TQ, TQ)
        vth = jnp.concatenate([vt_ref[h * FOX_HEAD_DIM:(h + 1) * FOX_HEAD_DIM, pl.ds(k0, TQ)],
                               jnp.ones((16, TQ), jnp.bfloat16)], axis=0)
        acc[h][...] = al[h][...] * acc[h][...] + _dot(vth, p_[h][...])

    @pl.when((pl.program_id(0) == 0) & (pl.program_id(1) == 0) & (qi == 0))
    def _():
        for h in range(nh):
            p_[h][...] = jnp.zeros_like(p_[h])

    load_queries(q_ref, fq_ref)
    for h in range(nh):
        scores(j0, h)

    for h in range(nh):
        acc[h][...] = jnp.zeros_like(acc[h])
        m_[h][...] = jnp.full(m_[h].shape, NEG_BIG, jnp.float32)
        al[h][...] = jnp.ones_like(al[h])

    def body(j, carry):
        prev = jnp.maximum(j - 1, j0)
        for h in range(nh):
            update(prev, h)
            softmax(h, False)
            scores(j + 1, h)
        return carry

    lax.fori_loop(j0, qi, body, 0)
    prev = jnp.maximum(qi - 1, j0)
    for h in range(nh):
        update(prev, h)
        softmax(h, True)
    for h in range(nh):
        update(qi, h)
    for pair in range(nh // 2):
        o_t = jnp.concatenate([a[0:FOX_HEAD_DIM, :] / a[FOX_HEAD_DIM:FOX_HEAD_DIM + 1, :]
                               for a in (acc[2 * pair][...], acc[2 * pair + 1][...])], axis=0)
        o_ref[:, pair * LANES:(pair + 1) * LANES] = o_t.T.astype(jnp.bfloat16)


def _first_blocks(nrm, fedge, batch, seq):
    nq = seq // TQ
    groups = FOX_HEADS // FOX_HEADS_PER_STEP
    heads = slice(0, FOX_HEADS)
    norms = jnp.sqrt(nrm.reshape(batch, nq, 2, LANES)[..., heads])
    qn, kn = norms[:, :, 0], norms[:, :, 1]
    fe = fedge.reshape(batch, nq, TQ // TM_MIX, 2, LANES)[..., FF_LANE0:FF_LANE0 + FOX_HEADS]
    f_start, f_end = fe[:, :, 0, 0], fe[:, :, -1, 1]
    bound = (qn[:, :, None] * (kn[:, None, :] + kn[:, :, None])
             + f_start[:, :, None] - f_end[:, None, :])
    j_idx = lax.broadcasted_iota(jnp.int32, bound.shape, 2)
    i_idx = lax.broadcasted_iota(jnp.int32, bound.shape, 1)
    skippable = (bound <= -PRUNE_LOG2) & (j_idx != i_idx)
    first_h = jnp.min(jnp.where(skippable, nq, j_idx), axis=2)
    first = jnp.min(first_h.reshape(batch, nq, groups, FOX_HEADS_PER_STEP), axis=-1)
    return first.transpose(0, 2, 1).reshape(-1)


def _fox(first, q, fqk, k, vt, batch, seq):
    nq = seq // TQ
    nh = FOX_HEADS_PER_STEP
    width = nh * FOX_HEAD_DIM
    groups = FOX_HEADS // nh
    qspec = pl.BlockSpec((TQ, width), lambda b, g, i, first: (b * nq + i, g))
    kspec = pl.BlockSpec((seq, width), lambda b, g, i, first: (b, g))
    vspec = pl.BlockSpec((None, width, seq), lambda b, g, i, first: (b, g, 0))
    fq_spec = pl.BlockSpec((TQ, LANES), lambda b, g, i, first: (b * nq + i, 0))
    fk_spec = pl.BlockSpec((seq, LANES), lambda b, g, i, first: (b, 1))
    acc_rows = FOX_HEAD_DIM + 16
    return pl.pallas_call(
        _fox_kernel,
        grid_spec=pltpu.PrefetchScalarGridSpec(
            num_scalar_prefetch=1,
            grid=(batch, groups, nq),
            in_specs=[qspec, fq_spec, kspec, fk_spec, vspec],
            out_specs=qspec,
            scratch_shapes=([pltpu.VMEM((2 * LANES, TQ), jnp.bfloat16)] * nh
                            + [pltpu.VMEM((TQ, TQ), jnp.float32)] * nh
                            + [pltpu.VMEM((TQ, TQ), jnp.bfloat16)] * nh
                            + [pltpu.VMEM((1, TQ), jnp.float32)] * (3 * nh)
                            + [pltpu.VMEM((acc_rows, TQ), jnp.float32)] * nh)),
        out_shape=jax.ShapeDtypeStruct(q.shape, jnp.bfloat16),
        compiler_params=pltpu.CompilerParams(
            dimension_semantics=("arbitrary", "arbitrary", "arbitrary"),
            vmem_limit_bytes=VMEM_LIMIT),
        name="fox",
    )(first, q, fqk, k, fqk, vt)


def _outffn_kernel(x_ref, mpg_ref, yf_ref, wo_ref, ln2_ref, wgu_ref, wdn_ref, lnf_ref, o_ref, *, final):
    half = POOL_WIDTH + GLA_WIDTH
    x1 = (x_ref[...] + _dot(mpg_ref[...], wo_ref[0:half, :]) + _dot(yf_ref[...], wo_ref[half:, :]))
    ms = jnp.mean(x1 * x1, axis=-1, keepdims=True)
    h = (x1 * lax.rsqrt(ms + EPS) * ln2_ref[...]).astype(jnp.bfloat16)
    gate = _dot(h, wgu_ref[:, 0:D_FF])
    up = _dot(h, wgu_ref[:, D_FF:])
    act = (gate * jax.nn.sigmoid(gate) * up).astype(jnp.bfloat16)
    x2 = x1 + _dot(act, wdn_ref[...])
    if final:
        ms2 = jnp.mean(x2 * x2, axis=-1, keepdims=True)
        x2 = x2 * lax.rsqrt(ms2 + EPS) * lnf_ref[...]
    o_ref[...] = x2


def _outffn(x2, mpg, yf, wo, ln2, wgu, wdn, lnf, final):
    bt = x2.shape[0]
    tm = TM_FFN
    row = lambda w: pl.BlockSpec((tm, w), lambda i: (i, 0))
    single = lambda shape: pl.BlockSpec(shape, lambda i: (0, 0), pipeline_mode=pl.Buffered(1))
    return pl.pallas_call(
        functools.partial(_outffn_kernel, final=final),
        grid=(bt // tm,),
        in_specs=[row(D_MODEL), row(POOL_WIDTH + GLA_WIDTH), row(FOX_WIDTH),
                  single(wo.shape), _const_spec((1, D_MODEL)), single(wgu.shape), single(wdn.shape),
                  _const_spec((1, D_MODEL))],
        out_specs=row(D_MODEL),
        out_shape=jax.ShapeDtypeStruct((bt, D_MODEL), jnp.float32),
        compiler_params=pltpu.CompilerParams(dimension_semantics=("arbitrary",),
                                             vmem_limit_bytes=VMEM_LIMIT),
        name="outffn",
    )(x2, mpg, yf, wo, ln2, wgu, wdn, lnf)


def _placement(dtype=jnp.bfloat16):
    place = np.zeros((3, LANES, 2 * LANES), np.float32)
    const = np.zeros((1, 2 * LANES), np.float32)
    for h in range(FOX_HEADS):
        qbase, kbase = h * F_SLOT, LANES + h * F_SLOT
        for part in range(3):
            place[part, FF_LANE0 + h, qbase + part] = 1.0
            place[part, FF_LANE0 + h, kbase + 3 + part] = -1.0
            const[0, qbase + 3 + part] = 1.0
            const[0, kbase + part] = 1.0
    return jnp.asarray(place, dtype), jnp.asarray(const)


def _triangles(dtype=jnp.bfloat16):
    r = np.arange(TM_MIX)[:, None]
    c = np.arange(TM_MIX)[None, :]
    full = (r >= c)
    chunk = full & (r // CHUNK == c // CHUNK)
    return jnp.asarray(chunk, dtype), jnp.asarray(full, dtype)


@jax.jit
def kernel(x, ln1, w_in, w_pool, pool_scale, w_a_up, b_a, gla_gn, b_f, w_o, ln2, w_gu, w_down, ln_f):
    batch, seq, _ = x.shape
    depth = w_in.shape[0]
    bf16 = jnp.bfloat16
    xf = x.reshape(batch * seq, D_MODEL)

    tri_chunk, tri_full = _triangles()
    place, place_const = _placement()
    group_of = np.arange(POOL_WIDTH) // POOL_GROUP
    same_group = jnp.asarray(group_of[:, None] == group_of[None, :])
    head_sum = jnp.asarray(np.arange(FOX_WIDTH)[:, None] // FOX_HEAD_DIM == np.arange(LANES)[None, :], bf16)
    assert TM_IN == TQ and TQ % TM_MIX == 0
    gla_head = np.arange(GLA_WIDTH) // GLA_HEAD_DIM
    head_mean = jnp.asarray((gla_head[:, None] == gla_head[None, :]) / GLA_HEAD_DIM, bf16)

    o0 = POOL_WIDTH
    o1 = o0 + 4 * GLA_WIDTH
    o2 = o1 + GLA_GATE_RANK
    o3 = o2 + 3 * FOX_WIDTH
    col_scale = np.ones((o3 + FOX_HEADS,), np.float32)
    col_scale[o0:o0 + GLA_WIDTH] = GLA_HEAD_DIM ** -0.5
    col_scale[o2:o2 + FOX_WIDTH] = FOX_HEAD_DIM ** -0.5 * LOG2E
    for l in range(depth):
        w = (w_in[l] * col_scale).astype(bf16)
        wu = w[:, 0:o0]
        wg = w[:, o0:o1]
        ws = jnp.pad(jnp.concatenate([w[:, o1:o2], w[:, o3:o3 + FOX_HEADS]], axis=1),
                     ((0, 0), (0, LANES - GLA_GATE_RANK - FOX_HEADS)))
        wq = w[:, o2:o2 + FOX_WIDTH]
        wk = w[:, o2 + FOX_WIDTH:o2 + 2 * FOX_WIDTH]
        wvt = w[:, o2 + 2 * FOX_WIDTH:o3].T
        u, gla, small, q, k, vt, nrm = _inproj(xf, ln1[l][None, :], wu, wg, ws, wq, wk, wvt, head_sum,
                                               batch, seq)

        wp = w_pool[l].reshape(POOL_WIDTH, POOL_GROUP)
        wpool_bd = jnp.where(same_group, jnp.tile(wp, (1, len(POOL_WINDOWS))), 0.0).astype(bf16)
        wa = jnp.pad(w_a_up[l], ((0, LANES - GLA_GATE_RANK), (0, 0)))
        wa_hi = wa.astype(bf16)
        wa_lo = (wa - wa_hi.astype(jnp.float32)).astype(bf16)
        bf_row = jnp.pad(b_f[l][None, :], ((0, 0), (FF_LANE0, LANES - FF_LANE0 - FOX_HEADS)))
        consts = [wpool_bd, pool_scale[l][None, :], wa_hi, wa_lo, b_a[l][None, :],
                  gla_gn[l][None, :], bf_row, tri_chunk, tri_full, place, place_const, head_mean]
        mpg, fqk, fedge = _mixpg(u, gla, small, consts, batch, seq)

        yf = _fox(_first_blocks(nrm, fedge, batch, seq), q, fqk, k, vt, batch, seq)

        xf = _outffn(xf, mpg, yf, w_o[l].astype(bf16), ln2[l][None, :], w_gu[l].astype(bf16),
                     w_down[l].astype(bf16), ln_f[None, :], final=(l == depth - 1))
    return xf.reshape(batch, seq, D_MODEL)
```

```python
import functools

import jax
import jax.numpy as jnp
import numpy as np
from jax import lax
from jax.experimental import pallas as pl
from jax.experimental.pallas import tpu as pltpu

D_MODEL = 1024
CHUNK = 64
EPS = 1e-6
POOL_WIDTH = 256
POOL_WINDOWS = (2, 4, 8, 16)
POOL_GROUP = 64
GLA_WIDTH = 256
GLA_HEADS = 4
GLA_HEAD_DIM = 64
GLA_GATE_RANK = 16
GLA_TAU = 16.0
FOX_WIDTH = 512
FOX_HEAD_DIM = 64
FOX_HEADS = 8
D_FF = 2816

LANES = 128
HALO = 32
FF_LANE0 = GLA_GATE_RANK
F_SLOT = 8
NEG_BIG = -1e30
LOG2E = 1.4426950408889634
PRUNE_LOG2 = 150.0
NORM_SLACK = 1.01

VMEM_LIMIT = 56 * 1024 * 1024

TM_IN = 512
TM_MIX = 256
MIX_SUBTILES = 8
TQ = 512
FOX_HEADS_PER_STEP = 8
TM_FFN = 512


def _dot(a, b):
    return jnp.dot(a, b, preferred_element_type=jnp.float32)


def _dot_nt(a, b):
    return lax.dot_general(a, b, (((1,), (1,)), ((), ())), preferred_element_type=jnp.float32)


def _dot_tn(a, b):
    return lax.dot_general(a, b, (((0,), (0,)), ((), ())), preferred_element_type=jnp.float32)


def _split3(x):
    hi = x.astype(jnp.bfloat16)
    r = x - hi.astype(jnp.float32)
    mid = r.astype(jnp.bfloat16)
    lo = (r - mid.astype(jnp.float32)).astype(jnp.bfloat16)
    return hi, mid, lo


def _log_sigmoid(x):
    return jnp.minimum(x, 0.0) - jnp.log(1.0 + jnp.exp(-jnp.abs(x)))


def _const_spec(shape):
    nd = len(shape)
    return pl.BlockSpec(shape, lambda *_: (0,) * nd)


def _inproj_kernel(x_ref, ln_ref, wu_ref, wg_ref, ws_ref, wq_ref, wk_ref, wvt_ref, hsum_ref,
                   u_ref, gla_ref, small_ref, q_ref, k_ref, vt_ref, nrm_ref):
    x = x_ref[...]
    ms = jnp.mean(x * x, axis=-1, keepdims=True)
    h = (x * lax.rsqrt(ms + EPS) * ln_ref[...]).astype(jnp.bfloat16)
    u_ref[...] = _dot(h, wu_ref[...])
    gla_ref[...] = _dot(h, wg_ref[...]).astype(jnp.bfloat16)
    small_ref[...] = _dot(h, ws_ref[...])
    qb = _dot(h, wq_ref[...]).astype(jnp.bfloat16)
    kb = _dot(h, wk_ref[...]).astype(jnp.bfloat16)
    q_ref[...] = qb
    k_ref[...] = kb
    vt_ref[...] = _dot_nt(wvt_ref[...], h).astype(jnp.bfloat16)

    def max_sq_norm(zb):
        z = zb.astype(jnp.float32)
        n2 = _dot((z * z).astype(jnp.bfloat16), hsum_ref[...])
        return jnp.max(n2, axis=0, keepdims=True) * NORM_SLACK

    nrm_ref[...] = jnp.concatenate([max_sq_norm(qb), max_sq_norm(kb)], axis=0)


def _inproj(x2, ln, wu, wg, ws, wq, wk, wvt, hsum, batch, seq):
    bt = x2.shape[0]
    tm = TM_IN
    nt = seq // tm
    row = lambda w: pl.BlockSpec((tm, w), lambda i: (i, 0))
    return pl.pallas_call(
        _inproj_kernel,
        grid=(bt // tm,),
        in_specs=[row(D_MODEL), _const_spec((1, D_MODEL)),
                  _const_spec(wu.shape), _const_spec(wg.shape), _const_spec(ws.shape),
                  _const_spec(wq.shape), _const_spec(wk.shape), _const_spec(wvt.shape),
                  _const_spec(hsum.shape)],
        out_specs=[row(POOL_WIDTH), row(4 * GLA_WIDTH), row(LANES), row(FOX_WIDTH), row(FOX_WIDTH),
                   pl.BlockSpec((None, FOX_WIDTH, tm), lambda i: (i // nt, 0, i % nt)),
                   pl.BlockSpec((None, 2, LANES), lambda i: (i, 0, 0))],
        out_shape=[jax.ShapeDtypeStruct((bt, POOL_WIDTH), jnp.float32),
                   jax.ShapeDtypeStruct((bt, 4 * GLA_WIDTH), jnp.bfloat16),
                   jax.ShapeDtypeStruct((bt, LANES), jnp.float32),
                   jax.ShapeDtypeStruct((bt, FOX_WIDTH), jnp.bfloat16),
                   jax.ShapeDtypeStruct((bt, FOX_WIDTH), jnp.bfloat16),
                   jax.ShapeDtypeStruct((batch, FOX_WIDTH, seq), jnp.bfloat16),
                   jax.ShapeDtypeStruct((bt // tm, 2, LANES), jnp.float32)],
        compiler_params=pltpu.CompilerParams(dimension_semantics=("arbitrary",),
                                             vmem_limit_bytes=VMEM_LIMIT),
        name="inproj",
    )(x2, ln, wu, wg, ws, wq, wk, wvt, hsum)


def _mixpg_kernel(u_ref, gla_ref, small_ref,
                  wpool_ref, pscale_ref, wa_hi_ref, wa_lo_ref, ba_ref, gn_ref, bf_ref,
                  tri_chunk_ref, tri_full_ref, place_ref, const_ref, hmean_ref,
                  mpg_ref, fqk_ref, fedge_ref,
                  ubuf, fcarry, st_ref, *bufs):
    ti = pl.program_id(1)
    assert POOL_WINDOWS == tuple(2 ** (n + 1) for n in range(len(POOL_WINDOWS)))
    assert HALO == 8 * len(POOL_WINDOWS)

    @pl.when(ti == 0)
    def _():
        ubuf[0:HALO, :] = jnp.zeros((HALO, POOL_WIDTH), jnp.float32)
        fcarry[...] = jnp.zeros_like(fcarry)
        st_ref[...] = jnp.zeros_like(st_ref)

    refs = (u_ref, gla_ref, small_ref, wpool_ref, pscale_ref, wa_hi_ref, wa_lo_ref, ba_ref, gn_ref, bf_ref,
            tri_chunk_ref, tri_full_ref, place_ref, const_ref, hmean_ref,
            mpg_ref, fqk_ref, fedge_ref, ubuf, fcarry, st_ref)
    tiles = [_mix_tile(ti * MIX_SUBTILES + sub, sub, *refs, bufs[sub], bufs[MIX_SUBTILES:])
             for sub in range(MIX_SUBTILES)]
    done = object()
    while tiles:
        tiles = [t for t in tiles if next(t, done) is not done]


def _mix_tile(tile_idx, sub, u_ref, gla_ref, small_ref,
              wpool_ref, pscale_ref, wa_hi_ref, wa_lo_ref, ba_ref, gn_ref, bf_ref,
              tri_chunk_ref, tri_full_ref, place_ref, const_ref, hmean_ref,
              mpg_ref, fqk_ref, fedge_ref, ubuf, fcarry, st_ref, bcum_ref, pool_bufs):
    tm = TM_MIX
    rows = slice(sub * tm, (sub + 1) * tm)

    u = u_ref[rows, :]
    ubuf[HALO:HALO + tm, :] = u
    total = HALO + tm
    prev = ubuf
    sums = {}
    for n, w in enumerate(POOL_WINDOWS):
        shift, start = w // 2, 8 * (n + 1)
        val = prev[start:total, :] + prev[start - shift:total - shift, :]
        sums[w] = val[HALO - start:, :]
        if n + 1 < len(POOL_WINDOWS):
            pool_bufs[n][start:total, :] = val
            prev = pool_bufs[n]
    lane = lax.broadcasted_iota(jnp.int32, (tm, POOL_WIDTH), 1)
    tpos = tile_idx * tm + lax.broadcasted_iota(jnp.int32, (tm, POOL_WIDTH), 0)
    wsum = sums[POOL_WINDOWS[-1]]
    width = jnp.full((tm, POOL_WIDTH), POOL_WINDOWS[-1], jnp.int32)
    for gi in range(len(POOL_WINDOWS) - 2, -1, -1):
        in_group = lane < (gi + 1) * POOL_GROUP
        wsum = jnp.where(in_group, sums[POOL_WINDOWS[gi]], wsum)
        width = jnp.where(in_group, POOL_WINDOWS[gi], width)
    cnt = jnp.minimum(tpos + 1, width).astype(jnp.float32)
    d = wsum / cnt - u
    y_pool = _dot(d.astype(jnp.bfloat16), wpool_ref[...]) * pscale_ref[...]
    mpg_ref[rows, 0:POOL_WIDTH] = y_pool.astype(jnp.bfloat16)
    ubuf[0:HALO, :] = ubuf[tm:tm + HALO, :]
    yield

    small = small_ref[rows, :]
    s_hi = small.astype(jnp.bfloat16)
    s_lo = (small - s_hi.astype(jnp.float32)).astype(jnp.bfloat16)
    a = (_dot(s_hi, wa_hi_ref[...]) + _dot(s_lo, wa_hi_ref[...]) + _dot(s_hi, wa_lo_ref[...])
         + ba_ref[...])
    la = _log_sigmoid(a) / GLA_TAU
    la_hi, la_mid, la_lo = _split3(la)
    tri_c = tri_chunk_ref[...]
    bcum_ref[...] = _dot(tri_c, la_hi) + _dot(tri_c, la_mid) + _dot(tri_c, la_lo)
    yield

    slane = lax.broadcasted_iota(jnp.int32, (tm, LANES), 1)
    is_ff = (slane >= FF_LANE0) & (slane < FF_LANE0 + FOX_HEADS)
    lf = jnp.where(is_ff, _log_sigmoid(small + bf_ref[...]), 0.0)
    lf_hi, lf_mid, lf_lo = _split3(lf)
    tri_f = tri_full_ref[...]
    fcum = _dot(tri_f, lf_hi) + _dot(tri_f, lf_mid) + _dot(tri_f, lf_lo) + fcarry[...]
    fcarry[...] = fcum[tm - 1:tm, :]
    fl = fcum * LOG2E
    fedge_ref[sub] = jnp.concatenate([fl[0:1, :], fl[tm - 1:tm, :]], axis=0)
    f_hi, f_mid, f_lo = _split3(fl)
    fqk_ref[rows, :] = (_dot(f_hi, place_ref[0]) + _dot(f_mid, place_ref[1]) + _dot(f_lo, place_ref[2])
                        + const_ref[...]).astype(jnp.bfloat16)
    yield

    glane = lax.broadcasted_iota(jnp.int32, (1, GLA_WIDTH), 1)
    head_masks = [(glane >= h * GLA_HEAD_DIM) & (glane < (h + 1) * GLA_HEAD_DIM)
                  for h in range(GLA_HEADS)]
    r_i = lax.broadcasted_iota(jnp.int32, (GLA_WIDTH, GLA_WIDTH), 0) // GLA_HEAD_DIM
    c_i = lax.broadcasted_iota(jnp.int32, (GLA_WIDTH, GLA_WIDTH), 1) // GLA_HEAD_DIM
    same_head = r_i == c_i
    n_chunks = tm // CHUNK

    def chunk_row(offset):
        return jnp.concatenate(
            [jnp.broadcast_to(bcum_ref[c * CHUNK + offset:c * CHUNK + offset + 1, :], (CHUNK, GLA_WIDTH))
             for c in range(n_chunks)], axis=0)

    q = gla_ref[rows, 0:GLA_WIDTH].astype(jnp.float32)
    k = gla_ref[rows, GLA_WIDTH:2 * GLA_WIDTH].astype(jnp.float32)
    v = gla_ref[rows, 2 * GLA_WIDTH:3 * GLA_WIDTH]
    bc = bcum_ref[...]
    b_mid = chunk_row(CHUNK // 2 - 1)
    b_last = chunk_row(CHUNK - 1)
    q_in = (q * jnp.exp(bc - b_mid)).astype(jnp.bfloat16)
    k_in = (k * jnp.exp(b_mid - bc)).astype(jnp.bfloat16)
    k_kv = (k * jnp.exp(b_last - bc)).astype(jnp.bfloat16)
    q_st = (q * jnp.exp(bc)).astype(jnp.bfloat16)
    yield

    in_chunk_causal = tri_c > 0
    att = jnp.concatenate(
        [jnp.where(in_chunk_causal,
                   _dot_nt(jnp.where(head_masks[h], q_in, jnp.zeros_like(q_in)), k_in),
                   0.0).astype(jnp.bfloat16) for h in range(GLA_HEADS)], axis=1)
    v_heads = jnp.concatenate([jnp.where(head_masks[h], v, jnp.zeros_like(v))
                               for h in range(GLA_HEADS)], axis=0)
    o_intra = _dot(att, v_heads)
    chunk_rows = [slice(c * CHUNK, (c + 1) * CHUNK) for c in range(n_chunks)]
    kv_t = [jnp.where(same_head, _dot_tn(v[cr], k_kv[cr]), 0.0) for cr in chunk_rows]
    yield

    st = st_ref[...]
    o_inter = []
    for c, cr in enumerate(chunk_rows):
        o_inter.append(_dot_nt(q_st[cr], st.astype(jnp.bfloat16)))
        dec = jnp.exp(bcum_ref[(c + 1) * CHUNK - 1:(c + 1) * CHUNK, :])
        st = st * dec + kv_t[c]
    st_ref[...] = st
    yield

    o = o_intra + jnp.concatenate(o_inter, axis=0)
    o2 = o * o
    o2_hi = o2.astype(jnp.bfloat16)
    o2_lo = (o2 - o2_hi.astype(jnp.float32)).astype(jnp.bfloat16)
    mean_sq = _dot(o2_hi, hmean_ref[...]) + _dot(o2_lo, hmean_ref[...])
    g = gla_ref[rows, 3 * GLA_WIDTH:4 * GLA_WIDTH].astype(jnp.float32)
    y_gla = o * lax.rsqrt(mean_sq + EPS) * gn_ref[...] * (g * jax.nn.sigmoid(g))
    mpg_ref[rows, POOL_WIDTH:POOL_WIDTH + GLA_WIDTH] = y_gla.astype(jnp.bfloat16)


def _mixpg(u, gla, small, consts, batch, seq):
    tm = TM_MIX
    step = MIX_SUBTILES * tm
    nt = seq // step
    row = lambda w: pl.BlockSpec((step, w), lambda b, i: (b * nt + i, 0))
    bt = u.shape[0]
    return pl.pallas_call(
        _mixpg_kernel,
        grid=(batch, nt),
        in_specs=[row(POOL_WIDTH), row(4 * GLA_WIDTH), row(LANES)] + [_const_spec(c.shape) for c in consts],
        out_specs=[row(POOL_WIDTH + GLA_WIDTH), row(2 * LANES),
                   pl.BlockSpec((MIX_SUBTILES, 2, LANES), lambda b, i: (b * nt + i, 0, 0))],
        out_shape=[jax.ShapeDtypeStruct((bt, POOL_WIDTH + GLA_WIDTH), jnp.bfloat16),
                   jax.ShapeDtypeStruct((bt, 2 * LANES), jnp.bfloat16),
                   jax.ShapeDtypeStruct((bt // tm, 2, LANES), jnp.float32)],
        scratch_shapes=[pltpu.VMEM((tm + HALO, POOL_WIDTH), jnp.float32),
                        pltpu.VMEM((1, LANES), jnp.float32),
                        pltpu.VMEM((GLA_WIDTH, GLA_WIDTH), jnp.float32)]
                       + [pltpu.VMEM((tm, GLA_WIDTH), jnp.float32)] * MIX_SUBTILES
                       + [pltpu.VMEM((tm + HALO, POOL_WIDTH), jnp.float32)] * (len(POOL_WINDOWS) - 1),
        compiler_params=pltpu.CompilerParams(dimension_semantics=("arbitrary", "arbitrary"),
                                             vmem_limit_bytes=VMEM_LIMIT),
        name="mixpg",
    )(u, gla, small, *consts)


def _fox_kernel(first_ref, q_ref, fq_ref, k_ref, fk_ref, vt_ref, o_ref, *scratch):
    nh = FOX_HEADS_PER_STEP
    qh, s_, p_, bm, m_, al, acc = (scratch[i * nh:(i + 1) * nh] for i in range(7))
    qi = pl.program_id(2)
    j0 = first_ref[(pl.program_id(0) * pl.num_programs(1) + pl.program_id(1)) * pl.num_programs(2) + qi]

    def load_queries(q_tile, fq_tile):
        row = lax.broadcasted_iota(jnp.int32, (2 * LANES, 1), 0)
        first_head = pl.program_id(1) * nh
        for pair in range(nh // 2):
            cols = slice(pair * LANES, (pair + 1) * LANES)
            qext_t = jnp.concatenate([q_tile[:, cols], fq_tile[...]], axis=1).astype(jnp.float32).T
            for sub in range(2):
                lo = sub * FOX_HEAD_DIM
                flo = LANES + (first_head + 2 * pair + sub) * F_SLOT
                own = ((row >= lo) & (row < lo + FOX_HEAD_DIM)) | ((row >= flo) & (row < flo + F_SLOT))
                qh[2 * pair + sub][...] = jnp.where(own, qext_t, 0.0).astype(jnp.bfloat16)

    def scores(j, h):
        k0 = pl.multiple_of(j * TQ, TQ)
        cols = slice((h // 2) * LANES, (h // 2 + 1) * LANES)
        kext = jnp.concatenate([k_ref[pl.ds(k0, TQ), cols], fk_ref[pl.ds(k0, TQ), :]], axis=1)
        s = _dot(kext, qh[h][...])
        s_[h][...] = s
        bm[h][...] = jnp.max(s, axis=0, keepdims=True)

    def softmax(h, diagonal):
        s = s_[h][...]
        if diagonal:
            krow = lax.broadcasted_iota(jnp.int32, (TQ, TQ), 0)
            qcol = lax.broadcasted_iota(jnp.int32, (TQ, TQ), 1)
            s = jnp.where(krow <= qcol, s, NEG_BIG)
            blk_max = jnp.max(s, axis=0, keepdims=True)
        else:
            blk_max = bm[h][...]
        m_old = m_[h][...]
        m_new = jnp.maximum(m_old, blk_max)
        al[h][...] = jnp.exp2(m_old - m_new)
        p_[h][...] = jnp.exp2((s - m_new).astype(jnp.bfloat16))
        m_[h][...] = m_new

    def update(j, h):
        k0 = pl.multiple_of(j * TQ, TQ)
        vth = jnp.concatenate([vt_ref[h * FOX_HEAD_DIM:(h + 1) * FOX_HEAD_DIM, pl.ds(k0, TQ)],
                               jnp.ones((16, TQ), jnp.bfloat16)], axis=0)
        acc[h][...] = al[h][...] * acc[h][...] + _dot(vth, p_[h][...])

    @pl.when((pl.program_id(0) == 0) & (pl.program_id(1) == 0) & (qi == 0))
    def _():
        for h in range(nh):
            p_[h][...] = jnp.zeros_like(p_[h])

    load_queries(q_ref, fq_ref)
    for h in range(nh):
        scores(j0, h)

    for h in range(nh):
        acc[h][...] = jnp.zeros_like(acc[h])
        m_[h][...] = jnp.full(m_[h].shape, NEG_BIG, jnp.float32)
        al[h][...] = jnp.ones_like(al[h])

    def body(j, carry):
        prev = jnp.maximum(j - 1, j0)
        for h in range(nh):
            update(prev, h)
            softmax(h, False)
            scores(j + 1, h)
        return carry

    lax.fori_loop(j0, qi, body, 0)
    prev = jnp.maximum(qi - 1, j0)
    for h in range(nh):
        update(prev, h)
        softmax(h, True)
    for h in range(nh):
        update(qi, h)
    for pair in range(nh // 2):
        o_t = jnp.concatenate([a[0:FOX_HEAD_DIM, :] / a[FOX_HEAD_DIM:FOX_HEAD_DIM + 1, :]
                               for a in (acc[2 * pair][...], acc[2 * pair + 1][...])], axis=0)
        o_ref[:, pair * LANES:(pair + 1) * LANES] = o_t.T.astype(jnp.bfloat16)


def _first_blocks(nrm, fedge, batch, seq):
    nq = seq // TQ
    groups = FOX_HEADS // FOX_HEADS_PER_STEP
    heads = slice(0, FOX_HEADS)
    norms = jnp.sqrt(nrm.reshape(batch, nq, 2, LANES)[..., heads])
    qn, kn = norms[:, :, 0], norms[:, :, 1]
    fe = fedge.reshape(batch, nq, TQ // TM_MIX, 2, LANES)[..., FF_LANE0:FF_LANE0 + FOX_HEADS]
    f_start, f_end = fe[:, :, 0, 0], fe[:, :, -1, 1]
    bound = (qn[:, :, None] * (kn[:, None, :] + kn[:, :, None])
             + f_start[:, :, None] - f_end[:, None, :])
    j_idx = lax.broadcasted_iota(jnp.int32, bound.shape, 2)
    i_idx = lax.broadcasted_iota(jnp.int32, bound.shape, 1)
    skippable = (bound <= -PRUNE_LOG2) & (j_idx != i_idx)
    first_h = jnp.min(jnp.where(skippable, nq, j_idx), axis=2)
    first = jnp.min(first_h.reshape(batch, nq, groups, FOX_HEADS_PER_STEP), axis=-1)
    return first.transpose(0, 2, 1).reshape(-1)


def _fox(first, q, fqk, k, vt, batch, seq):
    nq = seq // TQ
    nh = FOX_HEADS_PER_STEP
    width = nh * FOX_HEAD_DIM
    groups = FOX_HEADS // nh
    qspec = pl.BlockSpec((TQ, width), lambda b, g, i, first: (b * nq + i, g))
    kspec = pl.BlockSpec((seq, width), lambda b, g, i, first: (b, g))
    vspec = pl.BlockSpec((None, width, seq), lambda b, g, i, first: (b, g, 0))
    fq_spec = pl.BlockSpec((TQ, LANES), lambda b, g, i, first: (b * nq + i, 0))
    fk_spec = pl.BlockSpec((seq, LANES), lambda b, g, i, first: (b, 1))
    acc_rows = FOX_HEAD_DIM + 16
    return pl.pallas_call(
        _fox_kernel,
        grid_spec=pltpu.PrefetchScalarGridSpec(
            num_scalar_prefetch=1,
            grid=(batch, groups, nq),
            in_specs=[qspec, fq_spec, kspec, fk_spec, vspec],
            out_specs=qspec,
            scratch_shapes=([pltpu.VMEM((2 * LANES, TQ), jnp.bfloat16)] * nh
                            + [pltpu.VMEM((TQ, TQ), jnp.float32)] * nh
                            + [pltpu.VMEM((TQ, TQ), jnp.bfloat16)] * nh
                            + [pltpu.VMEM((1, TQ), jnp.float32)] * (3 * nh)
                            + [pltpu.VMEM((acc_rows, TQ), jnp.float32)] * nh)),
        out_shape=jax.ShapeDtypeStruct(q.shape, jnp.bfloat16),
        compiler_params=pltpu.CompilerParams(
            dimension_semantics=("arbitrary", "arbitrary", "arbitrary"),
            vmem_limit_bytes=VMEM_LIMIT),
        name="fox",
    )(first, q, fqk, k, fqk, vt)


def _outffn_kernel(x_ref, mpg_ref, yf_ref, wo_ref, ln2_ref, wgu_ref, wdn_ref, lnf_ref, o_ref, *, final):
    half = POOL_WIDTH + GLA_WIDTH
    x1 = (x_ref[...] + _dot(mpg_ref[...], wo_ref[0:half, :]) + _dot(yf_ref[...], wo_ref[half:, :]))
    ms = jnp.mean(x1 * x1, axis=-1, keepdims=True)
    h = (x1 * lax.rsqrt(ms + EPS) * ln2_ref[...]).astype(jnp.bfloat16)
    gate = _dot(h, wgu_ref[:, 0:D_FF])
    up = _dot(h, wgu_ref[:, D_FF:])
    act = (gate * jax.nn.sigmoid(gate) * up).astype(jnp.bfloat16)
    x2 = x1 + _dot(act, wdn_ref[...])
    if final:
        ms2 = jnp.mean(x2 * x2, axis=-1, keepdims=True)
        x2 = x2 * lax.rsqrt(ms2 + EPS) * lnf_ref[...]
    o_ref[...] = x2


def _outffn(x2, mpg, yf, wo, ln2, wgu, wdn, lnf, final):
    bt = x2.shape[0]
    tm = TM_FFN
    row = lambda w: pl.BlockSpec((tm, w), lambda i: (i, 0))
    single = lambda shape: pl.BlockSpec(shape, lambda i: (0, 0), pipeline_mode=pl.Buffered(1))
    return pl.pallas_call(
        functools.partial(_outffn_kernel, final=final),
        grid=(bt // tm,),
        in_specs=[row(D_MODEL), row(POOL_WIDTH + GLA_WIDTH), row(FOX_WIDTH),
                  single(wo.shape), _const_spec((1, D_MODEL)), single(wgu.shape), single(wdn.shape),
                  _const_spec((1, D_MODEL))],
        out_specs=row(D_MODEL),
        out_shape=jax.ShapeDtypeStruct((bt, D_MODEL), jnp.float32),
        compiler_params=pltpu.CompilerParams(dimension_semantics=("arbitrary",),
                                             vmem_limit_bytes=VMEM_LIMIT),
        name="outffn",
    )(x2, mpg, yf, wo, ln2, wgu, wdn, lnf)


def _placement(dtype=jnp.bfloat16):
    place = np.zeros((3, LANES, 2 * LANES), np.float32)
    const = np.zeros((1, 2 * LANES), np.float32)
    for h in range(FOX_HEADS):
        qbase, kbase = h * F_SLOT, LANES + h * F_SLOT
        for part in range(3):
            place[part, FF_LANE0 + h, qbase + part] = 1.0
            place[part, FF_LANE0 + h, kbase + 3 + part] = -1.0
            const[0, qbase + 3 + part] = 1.0
            const[0, kbase + part] = 1.0
    return jnp.asarray(place, dtype), jnp.asarray(const)


def _triangles(dtype=jnp.bfloat16):
    r = np.arange(TM_MIX)[:, None]
    c = np.arange(TM_MIX)[None, :]
    full = (r >= c)
    chunk = full & (r // CHUNK == c // CHUNK)
    return jnp.asarray(chunk, dtype), jnp.asarray(full, dtype)


@jax.jit
def kernel(x, ln1, w_in, w_pool, pool_scale, w_a_up, b_a, gla_gn, b_f, w_o, ln2, w_gu, w_down, ln_f):
    batch, seq, _ = x.shape
    depth = w_in.shape[0]
    bf16 = jnp.bfloat16
    xf = x.reshape(batch * seq, D_MODEL)

    tri_chunk, tri_full = _triangles()
    place, place_const = _placement()
    group_of = np.arange(POOL_WIDTH) // POOL_GROUP
    same_group = jnp.asarray(group_of[:, None] == group_of[None, :])
    head_sum = jnp.asarray(np.arange(FOX_WIDTH)[:, None] // FOX_HEAD_DIM == np.arange(LANES)[None, :], bf16)
    assert TM_IN == TQ and TQ % TM_MIX == 0
    gla_head = np.arange(GLA_WIDTH) // GLA_HEAD_DIM
    head_mean = jnp.asarray((gla_head[:, None] == gla_head[None, :]) / GLA_HEAD_DIM, bf16)

    o0 = POOL_WIDTH
    o1 = o0 + 4 * GLA_WIDTH
    o2 = o1 + GLA_GATE_RANK
    o3 = o2 + 3 * FOX_WIDTH
    col_scale = np.ones((o3 + FOX_HEADS,), np.float32)
    col_scale[o0:o0 + GLA_WIDTH] = GLA_HEAD_DIM ** -0.5
    col_scale[o2:o2 + FOX_WIDTH] = FOX_HEAD_DIM ** -0.5 * LOG2E
    for l in range(depth):
        w = (w_in[l] * col_scale).astype(bf16)
        wu = w[:, 0:o0]
        wg = w[:, o0:o1]
        ws = jnp.pad(jnp.concatenate([w[:, o1:o2], w[:, o3:o3 + FOX_HEADS]], axis=1),
                     ((0, 0), (0, LANES - GLA_GATE_RANK - FOX_HEADS)))
        wq = w[:, o2:o2 + FOX_WIDTH]
        wk = w[:, o2 + FOX_WIDTH:o2 + 2 * FOX_WIDTH]
        wvt = w[:, o2 + 2 * FOX_WIDTH:o3].T
        u, gla, small, q, k, vt, nrm = _inproj(xf, ln1[l][None, :], wu, wg, ws, wq, wk, wvt, head_sum,
                                               batch, seq)

        wp = w_pool[l].reshape(POOL_WIDTH, POOL_GROUP)
        wpool_bd = jnp.where(same_group, jnp.tile(wp, (1, len(POOL_WINDOWS))), 0.0).astype(bf16)
        wa = jnp.pad(w_a_up[l], ((0, LANES - GLA_GATE_RANK), (0, 0)))
        wa_hi = wa.astype(bf16)
        wa_lo = (wa - wa_hi.astype(jnp.float32)).astype(bf16)
        bf_row = jnp.pad(b_f[l][None, :], ((0, 0), (FF_LANE0, LANES - FF_LANE0 - FOX_HEADS)))
        consts = [wpool_bd, pool_scale[l][None, :], wa_hi, wa_lo, b_a[l][None, :],
                  gla_gn[l][None, :], bf_row, tri_chunk, tri_full, place, place_const, head_mean]
        mpg, fqk, fedge = _mixpg(u, gla, small, consts, batch, seq)

        yf = _fox(_first_blocks(nrm, fedge, batch, seq), q, fqk, k, vt, batch, seq)

        xf = _outffn(xf, mpg, yf, w_o[l].astype(bf16), ln2[l][None, :], w_gu[l].astype(bf16),
                     w_down[l].astype(bf16), ln_f[None, :], final=(l == depth - 1))
    return xf.reshape(batch, seq, D_MODEL)
```

```python
import functools

import jax
import jax.numpy as jnp
import numpy as np
from jax import lax
from jax.experimental import pallas as pl
from jax.experimental.pallas import tpu as pltpu

D_MODEL = 1024
CHUNK = 64
EPS = 1e-6
POOL_WIDTH = 256
POOL_WINDOWS = (2, 4, 8, 16)
POOL_GROUP = 64
GLA_WIDTH = 256
GLA_HEADS = 4
GLA_HEAD_DIM = 64
GLA_GATE_RANK = 16
GLA_TAU = 16.0
FOX_WIDTH = 512
FOX_HEAD_DIM = 64
FOX_HEADS = 8
D_FF = 2816

LANES = 128
HALO = 32
FF_LANE0 = GLA_GATE_RANK
F_SLOT = 8
NEG_BIG = -1e30
LOG2E = 1.4426950408889634
PRUNE_LOG2 = 150.0
NORM_SLACK = 1.01

VMEM_LIMIT = 56 * 1024 * 1024

TM_IN = 512
TM_MIX = 256
MIX_SUBTILES = 8
TQ = 512
FOX_STEP_BUNDLES = {4: 3806, 8: 6677}
FOX_TRIP_BUNDLES = {4: 1994, 8: 3660}
TM_FFN = 512


def _dot(a, b):
    return jnp.dot(a, b, preferred_element_type=jnp.float32)


def _dot_nt(a, b):
    return lax.dot_general(a, b, (((1,), (1,)), ((), ())), preferred_element_type=jnp.float32)


def _dot_tn(a, b):
    return lax.dot_general(a, b, (((0,), (0,)), ((), ())), preferred_element_type=jnp.float32)


def _split3(x):
    hi = x.astype(jnp.bfloat16)
    r = x - hi.astype(jnp.float32)
    mid = r.astype(jnp.bfloat16)
    lo = (r - mid.astype(jnp.float32)).astype(jnp.bfloat16)
    return hi, mid, lo


def _log_sigmoid(x):
    return jnp.minimum(x, 0.0) - jnp.log(1.0 + jnp.exp(-jnp.abs(x)))


def _const_spec(shape):
    nd = len(shape)
    return pl.BlockSpec(shape, lambda *_: (0,) * nd)


def _inproj_kernel(x_ref, ln_ref, wu_ref, wg_ref, ws_ref, wq_ref, wk_ref, wvt_ref, hsum_ref,
                   u_ref, gla_ref, small_ref, q_ref, k_ref, vt_ref, nrm_ref):
    x = x_ref[...]
    ms = jnp.mean(x * x, axis=-1, keepdims=True)
    h = (x * lax.rsqrt(ms + EPS) * ln_ref[...]).astype(jnp.bfloat16)
    u_ref[...] = _dot(h, wu_ref[...])
    gla_ref[...] = _dot(h, wg_ref[...]).astype(jnp.bfloat16)
    small_ref[...] = _dot(h, ws_ref[...])
    qb = _dot(h, wq_ref[...]).astype(jnp.bfloat16)
    kb = _dot(h, wk_ref[...]).astype(jnp.bfloat16)
    q_ref[...] = qb
    k_ref[...] = kb
    vt_ref[...] = _dot_nt(wvt_ref[...], h).astype(jnp.bfloat16)

    def max_sq_norm(zb):
        z = zb.astype(jnp.float32)
        n2 = _dot((z * z).astype(jnp.bfloat16), hsum_ref[...])
        return jnp.max(n2, axis=0, keepdims=True) * NORM_SLACK

    nrm_ref[...] = jnp.concatenate([max_sq_norm(qb), max_sq_norm(kb)], axis=0)


def _inproj(x2, ln, wu, wg, ws, wq, wk, wvt, hsum, batch, seq):
    bt = x2.shape[0]
    tm = TM_IN
    nt = seq // tm
    row = lambda w: pl.BlockSpec((tm, w), lambda i: (i, 0))
    return pl.pallas_call(
        _inproj_kernel,
        grid=(bt // tm,),
        in_specs=[row(D_MODEL), _const_spec((1, D_MODEL)),
                  _const_spec(wu.shape), _const_spec(wg.shape), _const_spec(ws.shape),
                  _const_spec(wq.shape), _const_spec(wk.shape), _const_spec(wvt.shape),
                  _const_spec(hsum.shape)],
        out_specs=[row(POOL_WIDTH), row(4 * GLA_WIDTH), row(LANES), row(FOX_WIDTH), row(FOX_WIDTH),
                   pl.BlockSpec((None, FOX_WIDTH, tm), lambda i: (i // nt, 0, i % nt)),
                   pl.BlockSpec((None, 2, LANES), lambda i: (i, 0, 0))],
        out_shape=[jax.ShapeDtypeStruct((bt, POOL_WIDTH), jnp.float32),
                   jax.ShapeDtypeStruct((bt, 4 * GLA_WIDTH), jnp.bfloat16),
                   jax.ShapeDtypeStruct((bt, LANES), jnp.float32),
                   jax.ShapeDtypeStruct((bt, FOX_WIDTH), jnp.bfloat16),
                   jax.ShapeDtypeStruct((bt, FOX_WIDTH), jnp.bfloat16),
                   jax.ShapeDtypeStruct((batch, FOX_WIDTH, seq), jnp.bfloat16),
                   jax.ShapeDtypeStruct((bt // tm, 2, LANES), jnp.float32)],
        compiler_params=pltpu.CompilerParams(dimension_semantics=("arbitrary",),
                                             vmem_limit_bytes=VMEM_LIMIT),
        name="inproj",
    )(x2, ln, wu, wg, ws, wq, wk, wvt, hsum)


def _mixpg_kernel(u_ref, gla_ref, small_ref,
                  wpool_ref, pscale_ref, wa_hi_ref, wa_lo_ref, ba_ref, gn_ref, bf_ref,
                  tri_chunk_ref, tri_full_ref, place_ref, const_ref, hmean_ref,
                  mpg_ref, fqk_ref, fedge_ref,
                  ubuf, fcarry, st_ref, *bufs):
    ti = pl.program_id(1)
    assert POOL_WINDOWS == tuple(2 ** (n + 1) for n in range(len(POOL_WINDOWS)))
    assert HALO == 8 * len(POOL_WINDOWS)

    @pl.when(ti == 0)
    def _():
        ubuf[0:HALO, :] = jnp.zeros((HALO, POOL_WIDTH), jnp.float32)
        fcarry[...] = jnp.zeros_like(fcarry)
        st_ref[...] = jnp.zeros_like(st_ref)

    refs = (u_ref, gla_ref, small_ref, wpool_ref, pscale_ref, wa_hi_ref, wa_lo_ref, ba_ref, gn_ref, bf_ref,
            tri_chunk_ref, tri_full_ref, place_ref, const_ref, hmean_ref,
            mpg_ref, fqk_ref, fedge_ref, ubuf, fcarry, st_ref)
    tiles = [_mix_tile(ti * MIX_SUBTILES + sub, sub, *refs, bufs[sub], bufs[MIX_SUBTILES:])
             for sub in range(MIX_SUBTILES)]
    done = object()
    while tiles:
        tiles = [t for t in tiles if next(t, done) is not done]


def _mix_tile(tile_idx, sub, u_ref, gla_ref, small_ref,
              wpool_ref, pscale_ref, wa_hi_ref, wa_lo_ref, ba_ref, gn_ref, bf_ref,
              tri_chunk_ref, tri_full_ref, place_ref, const_ref, hmean_ref,
              mpg_ref, fqk_ref, fedge_ref, ubuf, fcarry, st_ref, bcum_ref, pool_bufs):
    tm = TM_MIX
    rows = slice(sub * tm, (sub + 1) * tm)

    u = u_ref[rows, :]
    ubuf[HALO:HALO + tm, :] = u
    total = HALO + tm
    prev = ubuf
    sums = {}
    for n, w in enumerate(POOL_WINDOWS):
        shift, start = w // 2, 8 * (n + 1)
        val = prev[start:total, :] + prev[start - shift:total - shift, :]
        sums[w] = val[HALO - start:, :]
        if n + 1 < len(POOL_WINDOWS):
            pool_bufs[n][start:total, :] = val
            prev = pool_bufs[n]
    lane = lax.broadcasted_iota(jnp.int32, (tm, POOL_WIDTH), 1)
    tpos = tile_idx * tm + lax.broadcasted_iota(jnp.int32, (tm, POOL_WIDTH), 0)
    wsum = sums[POOL_WINDOWS[-1]]
    width = jnp.full((tm, POOL_WIDTH), POOL_WINDOWS[-1], jnp.int32)
    for gi in range(len(POOL_WINDOWS) - 2, -1, -1):
        in_group = lane < (gi + 1) * POOL_GROUP
        wsum = jnp.where(in_group, sums[POOL_WINDOWS[gi]], wsum)
        width = jnp.where(in_group, POOL_WINDOWS[gi], width)
    cnt = jnp.minimum(tpos + 1, width).astype(jnp.float32)
    d = wsum / cnt - u
    y_pool = _dot(d.astype(jnp.bfloat16), wpool_ref[...]) * pscale_ref[...]
    mpg_ref[rows, 0:POOL_WIDTH] = y_pool.astype(jnp.bfloat16)
    ubuf[0:HALO, :] = ubuf[tm:tm + HALO, :]
    yield

    small = small_ref[rows, :]
    s_hi = small.astype(jnp.bfloat16)
    s_lo = (small - s_hi.astype(jnp.float32)).astype(jnp.bfloat16)
    a = (_dot(s_hi, wa_hi_ref[...]) + _dot(s_lo, wa_hi_ref[...]) + _dot(s_hi, wa_lo_ref[...])
         + ba_ref[...])
    la = _log_sigmoid(a) / GLA_TAU
    la_hi, la_mid, la_lo = _split3(la)
    tri_c = tri_chunk_ref[...]
    bcum_ref[...] = _dot(tri_c, la_hi) + _dot(tri_c, la_mid) + _dot(tri_c, la_lo)
    yield

    slane = lax.broadcasted_iota(jnp.int32, (tm, LANES), 1)
    is_ff = (slane >= FF_LANE0) & (slane < FF_LANE0 + FOX_HEADS)
    lf = jnp.where(is_ff, _log_sigmoid(small + bf_ref[...]), 0.0)
    lf_hi, lf_mid, lf_lo = _split3(lf)
    tri_f = tri_full_ref[...]
    fcum = _dot(tri_f, lf_hi) + _dot(tri_f, lf_mid) + _dot(tri_f, lf_lo) + fcarry[...]
    fcarry[...] = fcum[tm - 1:tm, :]
    fl = fcum * LOG2E
    fedge_ref[sub] = jnp.concatenate([fl[0:1, :], fl[tm - 1:tm, :]], axis=0)
    f_hi, f_mid, f_lo = _split3(fl)
    fqk_ref[rows, :] = (_dot(f_hi, place_ref[0]) + _dot(f_mid, place_ref[1]) + _dot(f_lo, place_ref[2])
                        + const_ref[...]).astype(jnp.bfloat16)
    yield

    glane = lax.broadcasted_iota(jnp.int32, (1, GLA_WIDTH), 1)
    head_masks = [(glane >= h * GLA_HEAD_DIM) & (glane < (h + 1) * GLA_HEAD_DIM)
                  for h in range(GLA_HEADS)]
    r_i = lax.broadcasted_iota(jnp.int32, (GLA_WIDTH, GLA_WIDTH), 0) // GLA_HEAD_DIM
    c_i = lax.broadcasted_iota(jnp.int32, (GLA_WIDTH, GLA_WIDTH), 1) // GLA_HEAD_DIM
    same_head = r_i == c_i
    n_chunks = tm // CHUNK

    def chunk_row(offset):
        return jnp.concatenate(
            [jnp.broadcast_to(bcum_ref[c * CHUNK + offset:c * CHUNK + offset + 1, :], (CHUNK, GLA_WIDTH))
             for c in range(n_chunks)], axis=0)

    q = gla_ref[rows, 0:GLA_WIDTH].astype(jnp.float32)
    k = gla_ref[rows, GLA_WIDTH:2 * GLA_WIDTH].astype(jnp.float32)
    v = gla_ref[rows, 2 * GLA_WIDTH:3 * GLA_WIDTH]
    bc = bcum_ref[...]
    b_mid = chunk_row(CHUNK // 2 - 1)
    b_last = chunk_row(CHUNK - 1)
    q_in = (q * jnp.exp(bc - b_mid)).astype(jnp.bfloat16)
    k_in = (k * jnp.exp(b_mid - bc)).astype(jnp.bfloat16)
    k_kv = (k * jnp.exp(b_last - bc)).astype(jnp.bfloat16)
    q_st = (q * jnp.exp(bc)).astype(jnp.bfloat16)
    yield

    in_chunk_causal = tri_c > 0
    att = jnp.concatenate(
        [jnp.where(in_chunk_causal,
                   _dot_nt(jnp.where(head_masks[h], q_in, jnp.zeros_like(q_in)), k_in),
                   0.0).astype(jnp.bfloat16) for h in range(GLA_HEADS)], axis=1)
    v_heads = jnp.concatenate([jnp.where(head_masks[h], v, jnp.zeros_like(v))
                               for h in range(GLA_HEADS)], axis=0)
    o_intra = _dot(att, v_heads)
    chunk_rows = [slice(c * CHUNK, (c + 1) * CHUNK) for c in range(n_chunks)]
    kv_t = [jnp.where(same_head, _dot_tn(v[cr], k_kv[cr]), 0.0) for cr in chunk_rows]
    yield

    st = st_ref[...]
    o_inter = []
    for c, cr in enumerate(chunk_rows):
        o_inter.append(_dot_nt(q_st[cr], st.astype(jnp.bfloat16)))
        dec = jnp.exp(bcum_ref[(c + 1) * CHUNK - 1:(c + 1) * CHUNK, :])
        st = st * dec + kv_t[c]
    st_ref[...] = st
    yield

    o = o_intra + jnp.concatenate(o_inter, axis=0)
    o2 = o * o
    o2_hi = o2.astype(jnp.bfloat16)
    o2_lo = (o2 - o2_hi.astype(jnp.float32)).astype(jnp.bfloat16)
    mean_sq = _dot(o2_hi, hmean_ref[...]) + _dot(o2_lo, hmean_ref[...])
    g = gla_ref[rows, 3 * GLA_WIDTH:4 * GLA_WIDTH].astype(jnp.float32)
    y_gla = o * lax.rsqrt(mean_sq + EPS) * gn_ref[...] * (g * jax.nn.sigmoid(g))
    mpg_ref[rows, POOL_WIDTH:POOL_WIDTH + GLA_WIDTH] = y_gla.astype(jnp.bfloat16)


def _mixpg(u, gla, small, consts, batch, seq):
    tm = TM_MIX
    step = MIX_SUBTILES * tm
    nt = seq // step
    row = lambda w: pl.BlockSpec((step, w), lambda b, i: (b * nt + i, 0))
    bt = u.shape[0]
    return pl.pallas_call(
        _mixpg_kernel,
        grid=(batch, nt),
        in_specs=[row(POOL_WIDTH), row(4 * GLA_WIDTH), row(LANES)] + [_const_spec(c.shape) for c in consts],
        out_specs=[row(POOL_WIDTH + GLA_WIDTH), row(2 * LANES),
                   pl.BlockSpec((MIX_SUBTILES, 2, LANES), lambda b, i: (b * nt + i, 0, 0))],
        out_shape=[jax.ShapeDtypeStruct((bt, POOL_WIDTH + GLA_WIDTH), jnp.bfloat16),
                   jax.ShapeDtypeStruct((bt, 2 * LANES), jnp.bfloat16),
                   jax.ShapeDtypeStruct((bt // tm, 2, LANES), jnp.float32)],
        scratch_shapes=[pltpu.VMEM((tm + HALO, POOL_WIDTH), jnp.float32),
                        pltpu.VMEM((1, LANES), jnp.float32),
                        pltpu.VMEM((GLA_WIDTH, GLA_WIDTH), jnp.float32)]
                       + [pltpu.VMEM((tm, GLA_WIDTH), jnp.float32)] * MIX_SUBTILES
                       + [pltpu.VMEM((tm + HALO, POOL_WIDTH), jnp.float32)] * (len(POOL_WINDOWS) - 1),
        compiler_params=pltpu.CompilerParams(dimension_semantics=("arbitrary", "arbitrary"),
                                             vmem_limit_bytes=VMEM_LIMIT),
        name="mixpg",
    )(u, gla, small, *consts)


def _fox_kernel(first_ref, q_ref, fq_ref, k_ref, fk_ref, vt_ref, o_ref, *scratch, nh):
    qh, s_, p_, bm, m_, al, acc = (scratch[i * nh:(i + 1) * nh] for i in range(7))
    qi = pl.program_id(2)
    j0 = first_ref[(pl.program_id(0) * pl.num_programs(1) + pl.program_id(1)) * pl.num_programs(2) + qi]

    def load_queries(q_tile, fq_tile):
        row = lax.broadcasted_iota(jnp.int32, (2 * LANES, 1), 0)
        first_head = pl.program_id(1) * nh
        for pair in range(nh // 2):
            cols = slice(pair * LANES, (pair + 1) * LANES)
            qext_t = jnp.concatenate([q_tile[:, cols], fq_tile[...]], axis=1).astype(jnp.float32).T
            for sub in range(2):
                lo = sub * FOX_HEAD_DIM
                flo = LANES + (first_head + 2 * pair + sub) * F_SLOT
                own = ((row >= lo) & (row < lo + FOX_HEAD_DIM)) | ((row >= flo) & (row < flo + F_SLOT))
                qh[2 * pair + sub][...] = jnp.where(own, qext_t, 0.0).astype(jnp.bfloat16)

    def scores(j, h):
        k0 = pl.multiple_of(j * TQ, TQ)
        cols = slice((h // 2) * LANES, (h // 2 + 1) * LANES)
        kext = jnp.concatenate([k_ref[pl.ds(k0, TQ), cols], fk_ref[pl.ds(k0, TQ), :]], axis=1)
        s = _dot(kext, qh[h][...])
        s_[h][...] = s
        bm[h][...] = jnp.max(s, axis=0, keepdims=True)

    def softmax(h, diagonal):
        s = s_[h][...]
        if diagonal:
            krow = lax.broadcasted_iota(jnp.int32, (TQ, TQ), 0)
            qcol = lax.broadcasted_iota(jnp.int32, (TQ, TQ), 1)
            s = jnp.where(krow <= qcol, s, NEG_BIG)
            blk_max = jnp.max(s, axis=0, keepdims=True)
        else:
            blk_max = bm[h][...]
        m_old = m_[h][...]
        m_new = jnp.maximum(m_old, blk_max)
        al[h][...] = jnp.exp2(m_old - m_new)
        p_[h][...] = jnp.exp2((s - m_new).astype(jnp.bfloat16))
        m_[h][...] = m_new

    def update(j, h):
        k0 = pl.multiple_of(j * TQ, TQ)
        vth = jnp.concatenate([vt_ref[h * FOX_HEAD_DIM:(h + 1) * FOX_HEAD_DIM, pl.ds(k0, TQ)],
                               jnp.ones((16, TQ), jnp.bfloat16)], axis=0)
        acc[h][...] = al[h][...] * acc[h][...] + _dot(vth, p_[h][...])

    @pl.when((pl.program_id(0) == 0) & (pl.program_id(1) == 0) & (qi == 0))
    def _():
        for h in range(nh):
            p_[h][...] = jnp.zeros_like(p_[h])

    load_queries(q_ref, fq_ref)
    for h in range(nh):
        scores(j0, h)

    for h in range(nh):
        acc[h][...] = jnp.zeros_like(acc[h])
        m_[h][...] = jnp.full(m_[h].shape, NEG_BIG, jnp.float32)
        al[h][...] = jnp.ones_like(al[h])

    def body(j, carry):
        prev = jnp.maximum(j - 1, j0)
        for h in range(nh):
            update(prev, h)
            softmax(h, False)
            scores(j + 1, h)
        return carry

    lax.fori_loop(j0, qi, body, 0)
    prev = jnp.maximum(qi - 1, j0)
    for h in range(nh):
        update(prev, h)
        softmax(h, True)
    for h in range(nh):
        update(qi, h)
    for pair in range(nh // 2):
        o_t = jnp.concatenate([a[0:FOX_HEAD_DIM, :] / a[FOX_HEAD_DIM:FOX_HEAD_DIM + 1, :]
                               for a in (acc[2 * pair][...], acc[2 * pair + 1][...])], axis=0)
        o_ref[:, pair * LANES:(pair + 1) * LANES] = o_t.T.astype(jnp.bfloat16)


def _first_blocks(nrm, fedge, batch, seq, nh):
    nq = seq // TQ
    groups = FOX_HEADS // nh
    heads = slice(0, FOX_HEADS)
    norms = jnp.sqrt(nrm.reshape(batch, nq, 2, LANES)[..., heads])
    qn, kn = norms[:, :, 0], norms[:, :, 1]
    fe = fedge.reshape(batch, nq, TQ // TM_MIX, 2, LANES)[..., FF_LANE0:FF_LANE0 + FOX_HEADS]
    f_start, f_end = fe[:, :, 0, 0], fe[:, :, -1, 1]
    bound = (qn[:, :, None] * (kn[:, None, :] + kn[:, :, None])
             + f_start[:, :, None] - f_end[:, None, :])
    j_idx = lax.broadcasted_iota(jnp.int32, bound.shape, 2)
    i_idx = lax.broadcasted_iota(jnp.int32, bound.shape, 1)
    skippable = (bound <= -PRUNE_LOG2) & (j_idx != i_idx)
    first_h = jnp.min(jnp.where(skippable, nq, j_idx), axis=2)
    first = jnp.min(first_h.reshape(batch, nq, groups, nh), axis=-1)
    return first.transpose(0, 2, 1).reshape(-1)


def _fox(first, q, fqk, k, vt, batch, seq, nh):
    nq = seq // TQ
    width = nh * FOX_HEAD_DIM
    groups = FOX_HEADS // nh
    qspec = pl.BlockSpec((TQ, width), lambda b, g, i, first: (b * nq + i, g))
    kspec = pl.BlockSpec((seq, width), lambda b, g, i, first: (b, g))
    vspec = pl.BlockSpec((None, width, seq), lambda b, g, i, first: (b, g, 0))
    fq_spec = pl.BlockSpec((TQ, LANES), lambda b, g, i, first: (b * nq + i, 0))
    fk_spec = pl.BlockSpec((seq, LANES), lambda b, g, i, first: (b, 1))
    acc_rows = FOX_HEAD_DIM + 16
    return pl.pallas_call(
        functools.partial(_fox_kernel, nh=nh),
        grid_spec=pltpu.PrefetchScalarGridSpec(
            num_scalar_prefetch=1,
            grid=(batch, groups, nq),
            in_specs=[qspec, fq_spec, kspec, fk_spec, vspec],
            out_specs=qspec,
            scratch_shapes=([pltpu.VMEM((2 * LANES, TQ), jnp.bfloat16)] * nh
                            + [pltpu.VMEM((TQ, TQ), jnp.float32)] * nh
                            + [pltpu.VMEM((TQ, TQ), jnp.bfloat16)] * nh
                            + [pltpu.VMEM((1, TQ), jnp.float32)] * (3 * nh)
                            + [pltpu.VMEM((acc_rows, TQ), jnp.float32)] * nh)),
        out_shape=jax.ShapeDtypeStruct(q.shape, jnp.bfloat16),
        compiler_params=pltpu.CompilerParams(
            dimension_semantics=("arbitrary", "arbitrary", "arbitrary"),
            vmem_limit_bytes=VMEM_LIMIT),
        name="fox",
    )(first, q, fqk, k, fqk, vt)


def _outffn_kernel(x_ref, mpg_ref, yf_ref, wo_ref, ln2_ref, wgu_ref, wdn_ref, lnf_ref, o_ref, *, final):
    half = POOL_WIDTH + GLA_WIDTH
    x1 = (x_ref[...] + _dot(mpg_ref[...], wo_ref[0:half, :]) + _dot(yf_ref[...], wo_ref[half:, :]))
    ms = jnp.mean(x1 * x1, axis=-1, keepdims=True)
    h = (x1 * lax.rsqrt(ms + EPS) * ln2_ref[...]).astype(jnp.bfloat16)
    gate = _dot(h, wgu_ref[:, 0:D_FF])
    up = _dot(h, wgu_ref[:, D_FF:])
    act = (gate * jax.nn.sigmoid(gate) * up).astype(jnp.bfloat16)
    x2 = x1 + _dot(act, wdn_ref[...])
    if final:
        ms2 = jnp.mean(x2 * x2, axis=-1, keepdims=True)
        x2 = x2 * lax.rsqrt(ms2 + EPS) * lnf_ref[...]
    o_ref[...] = x2


def _outffn(x2, mpg, yf, wo, ln2, wgu, wdn, lnf, final):
    bt = x2.shape[0]
    tm = TM_FFN
    row = lambda w: pl.BlockSpec((tm, w), lambda i: (i, 0))
    single = lambda shape: pl.BlockSpec(shape, lambda i: (0, 0), pipeline_mode=pl.Buffered(1))
    return pl.pallas_call(
        functools.partial(_outffn_kernel, final=final),
        grid=(bt // tm,),
        in_specs=[row(D_MODEL), row(POOL_WIDTH + GLA_WIDTH), row(FOX_WIDTH),
                  single(wo.shape), _const_spec((1, D_MODEL)), single(wgu.shape), single(wdn.shape),
                  _const_spec((1, D_MODEL))],
        out_specs=row(D_MODEL),
        out_shape=jax.ShapeDtypeStruct((bt, D_MODEL), jnp.float32),
        compiler_params=pltpu.CompilerParams(dimension_semantics=("arbitrary",),
                                             vmem_limit_bytes=VMEM_LIMIT),
        name="outffn",
    )(x2, mpg, yf, wo, ln2, wgu, wdn, lnf)


def _placement(dtype=jnp.bfloat16):
    place = np.zeros((3, LANES, 2 * LANES), np.float32)
    const = np.zeros((1, 2 * LANES), np.float32)
    for h in range(FOX_HEADS):
        qbase, kbase = h * F_SLOT, LANES + h * F_SLOT
        for part in range(3):
            place[part, FF_LANE0 + h, qbase + part] = 1.0
            place[part, FF_LANE0 + h, kbase + 3 + part] = -1.0
            const[0, qbase + 3 + part] = 1.0
            const[0, kbase + part] = 1.0
    return jnp.asarray(place, dtype), jnp.asarray(const)


def _triangles(dtype=jnp.bfloat16):
    r = np.arange(TM_MIX)[:, None]
    c = np.arange(TM_MIX)[None, :]
    full = (r >= c)
    chunk = full & (r // CHUNK == c // CHUNK)
    return jnp.asarray(chunk, dtype), jnp.asarray(full, dtype)


@jax.jit
def kernel(x, ln1, w_in, w_pool, pool_scale, w_a_up, b_a, gla_gn, b_f, w_o, ln2, w_gu, w_down, ln_f):
    batch, seq, _ = x.shape
    depth = w_in.shape[0]
    bf16 = jnp.bfloat16
    xf = x.reshape(batch * seq, D_MODEL)

    tri_chunk, tri_full = _triangles()
    place, place_const = _placement()
    group_of = np.arange(POOL_WIDTH) // POOL_GROUP
    same_group = jnp.asarray(group_of[:, None] == group_of[None, :])
    head_sum = jnp.asarray(np.arange(FOX_WIDTH)[:, None] // FOX_HEAD_DIM == np.arange(LANES)[None, :], bf16)
    assert TM_IN == TQ and TQ % TM_MIX == 0
    gla_head = np.arange(GLA_WIDTH) // GLA_HEAD_DIM
    head_mean = jnp.asarray((gla_head[:, None] == gla_head[None, :]) / GLA_HEAD_DIM, bf16)

    o0 = POOL_WIDTH
    o1 = o0 + 4 * GLA_WIDTH
    o2 = o1 + GLA_GATE_RANK
    o3 = o2 + 3 * FOX_WIDTH
    col_scale = np.ones((o3 + FOX_HEADS,), np.float32)
    col_scale[o0:o0 + GLA_WIDTH] = GLA_HEAD_DIM ** -0.5
    col_scale[o2:o2 + FOX_WIDTH] = FOX_HEAD_DIM ** -0.5 * LOG2E
    for l in range(depth):
        w = (w_in[l] * col_scale).astype(bf16)
        wu = w[:, 0:o0]
        wg = w[:, o0:o1]
        ws = jnp.pad(jnp.concatenate([w[:, o1:o2], w[:, o3:o3 + FOX_HEADS]], axis=1),
                     ((0, 0), (0, LANES - GLA_GATE_RANK - FOX_HEADS)))
        wq = w[:, o2:o2 + FOX_WIDTH]
        wk = w[:, o2 + FOX_WIDTH:o2 + 2 * FOX_WIDTH]
        wvt = w[:, o2 + 2 * FOX_WIDTH:o3].T
        u, gla, small, q, k, vt, nrm = _inproj(xf, ln1[l][None, :], wu, wg, ws, wq, wk, wvt, head_sum,
                                               batch, seq)

        wp = w_pool[l].reshape(POOL_WIDTH, POOL_GROUP)
        wpool_bd = jnp.where(same_group, jnp.tile(wp, (1, len(POOL_WINDOWS))), 0.0).astype(bf16)
        wa = jnp.pad(w_a_up[l], ((0, LANES - GLA_GATE_RANK), (0, 0)))
        wa_hi = wa.astype(bf16)
        wa_lo = (wa - wa_hi.astype(jnp.float32)).astype(bf16)
        bf_row = jnp.pad(b_f[l][None, :], ((0, 0), (FF_LANE0, LANES - FF_LANE0 - FOX_HEADS)))
        consts = [wpool_bd, pool_scale[l][None, :], wa_hi, wa_lo, b_a[l][None, :],
                  gla_gn[l][None, :], bf_row, tri_chunk, tri_full, place, place_const, head_mean]
        mpg, fqk, fedge = _mixpg(u, gla, small, consts, batch, seq)

        nq = seq // TQ
        tile = jnp.arange(nq, dtype=jnp.int32)
        firsts, costs = {}, {}
        for nh in FOX_STEP_BUNDLES:
            firsts[nh] = _first_blocks(nrm, fedge, batch, seq, nh)
            trips = jnp.sum(tile[None, :] - firsts[nh].reshape(-1, nq))
            costs[nh] = firsts[nh].size * FOX_STEP_BUNDLES[nh] + trips * FOX_TRIP_BUNDLES[nh]
        yf = lax.cond(costs[8] < costs[4],
                      lambda: _fox(firsts[8], q, fqk, k, vt, batch, seq, 8),
                      lambda: _fox(firsts[4], q, fqk, k, vt, batch, seq, 4))

        xf = _outffn(xf, mpg, yf, w_o[l].astype(bf16), ln2[l][None, :], w_gu[l].astype(bf16),
                     w_down[l].astype(bf16), ln_f[None, :], final=(l == depth - 1))
    return xf.reshape(batch, seq, D_MODEL)
```
